```python
import math
import jax
import jax.numpy as jnp
from jax import lax
import numpy as np

D_MODEL = 2048
BATCH = 4
SEQ = 2048
DEPTH = 2
DEC_BATCH = 128
DEC_SEQ = 1
PAST_LEN = 2048
PAGE_SIZE = 128

A_WIDTH = D_MODEL // 2
A_HEAD_DIM = 64
A_HEADS = A_WIDTH // (2 * A_HEAD_DIM)
A_QK = 2 * A_HEAD_DIM
A_V = 2 * A_HEAD_DIM
B_WIDTH = D_MODEL - A_WIDTH
B_HEAD = 64
B_HEADS = B_WIDTH // B_HEAD
B_DECAY_LORA = max(32, int(round(math.sqrt(B_WIDTH) * 1.8 / 32)) * 32)
B_A_LORA = max(32, int(round(math.sqrt(B_WIDTH) * 1.8 / 32)) * 32)
B_G_LORA = max(32, int(round(B_WIDTH ** 0.8 * 0.6 / 32)) * 32)
B_PROJ = 3 * B_WIDTH + B_DECAY_LORA + B_A_LORA + B_G_LORA
IN0_COLS = 3 * A_WIDTH + B_PROJ
C_HEADS = 16
C_HEAD_DIM = D_MODEL // C_HEADS
N_GROUPS = 4
EXPERTS_PER_GROUP = 8
N_EXPERTS = N_GROUPS * EXPERTS_PER_GROUP
TOP_K = 2
EXPERT_HIDDEN = D_MODEL // 2
Q_BLOCK = 128
RMS_EPS = 1e-6
GN_EPS = 64e-5

kernel_name = 'hybrid_diffattn_rwkv7_stickbreak_hmoe_step'


def rmsnorm(x, g):
    xf = x.astype(jnp.float32)
    y = xf * lax.rsqrt(jnp.mean(xf * xf, axis=-1, keepdims=True) + RMS_EPS)
    return (y * g.astype(jnp.float32)).astype(x.dtype)


def alibi_slopes(n):
    return jnp.exp2(-8.0 * jnp.arange(1, n + 1, dtype=jnp.float32) / n)


def diff_attn_core(q, k, v, q_pos, k_pos, lam):
    q = q.reshape(q.shape[:3] + (2, A_HEAD_DIM))
    k = k.reshape(k.shape[:3] + (2, A_HEAD_DIM))
    s = jnp.einsum('bqhmd,bkhmd->bhmqk', q, k).astype(jnp.float32) * (A_HEAD_DIM ** -0.5)
    dist = q_pos[:, None] - k_pos[None, :]
    bias = -alibi_slopes(A_HEADS)[:, None, None] * jnp.abs(dist).astype(jnp.float32)
    s = jnp.where(dist >= 0, s + bias[None, :, None], -jnp.inf)
    p = jax.nn.softmax(s, axis=-1)
    w = p[:, :, 0] - lam * p[:, :, 1]
    return jnp.einsum('bhqk,bkhe->bqhe', w.astype(v.dtype), v)


def stick_breaking_core(q, k, v, q_pos, k_pos):
    z = jnp.einsum('bqhd,bkhd->bhqk', q, k).astype(jnp.float32) * (C_HEAD_DIM ** -0.5)
    mask = k_pos[None, :] < q_pos[:, None]
    log_beta = jax.nn.log_sigmoid(z)
    log_keep = jnp.where(mask, jax.nn.log_sigmoid(-z), 0.0)
    later = lax.cumsum(log_keep, axis=3, reverse=True) - log_keep
    att = jnp.where(mask, jnp.exp(log_beta + later), 0.0)
    return jnp.einsum('bhqk,bkhd->bqhd', att.astype(v.dtype), v)


def sweep_query_blocks(core, q):
    b, t = q.shape[0], q.shape[1]
    nb = t // Q_BLOCK
    qb = jnp.moveaxis(q.reshape((b, nb, Q_BLOCK) + q.shape[2:]), 1, 0)
    starts = jnp.arange(nb, dtype=jnp.int32) * Q_BLOCK
    out = lax.map(lambda a: core(a[0], a[1]), (qb, starts))
    out = jnp.moveaxis(out, 0, 1)
    return out.reshape((b, t) + out.shape[3:])


def gather_pages(cache, row):
    g = cache[row]
    return g.reshape((g.shape[0] * g.shape[1],) + g.shape[2:])


def paged_attend(core, q, k_new, v_new, cache_k, cache_v, page_table):
    past = page_table.shape[1] * cache_k.shape[1]
    t_new = q.shape[1]
    q_pos = past + jnp.arange(t_new, dtype=jnp.int32)
    k_pos = jnp.arange(past + t_new, dtype=jnp.int32)

    def one(args):
        qi, ki, vi, row = args
        k_all = jnp.concatenate([gather_pages(cache_k, row), ki], axis=0)
        v_all = jnp.concatenate([gather_pages(cache_v, row), vi], axis=0)
        return core(qi[None], k_all[None], v_all[None], q_pos, k_pos)[0]

    return lax.map(one, (q, k_new, v_new, page_table))


def rwkv7_time_mix(p, p_prev, s0, mu, w0, w2, a0, a2, g2, k_k, k_a, r_k, ln_w, ln_b):
    bsz, t = p.shape[0], p.shape[1]
    f32 = jnp.float32
    shifted = jnp.concatenate([p_prev, p[:, :-1]], axis=1)
    xs = p + (shifted - p) * mu
    o1, o2, o3 = B_WIDTH, 2 * B_WIDTH, 3 * B_WIDTH
    o4 = o3 + B_DECAY_LORA
    o5 = o4 + B_A_LORA
    r, k, v = xs[..., :o1], xs[..., o1:o2], xs[..., o2:o3]
    wl, al, gl = xs[..., o3:o4], xs[..., o4:o5], xs[..., o5:]
    w = -jax.nn.softplus(-(w0 + jnp.tanh(wl) @ w2).astype(f32)) - 0.5
    decay = jnp.exp(-jnp.exp(w))
    a = jax.nn.sigmoid((a0 + al @ a2).astype(f32))
    g = (jax.nn.sigmoid(gl) @ g2).astype(f32)

    def heads(z):
        return z.reshape(bsz, t, B_HEADS, B_HEAD)

    kk = heads((k * k_k).astype(f32))
    kk = kk * lax.rsqrt(jnp.maximum(jnp.sum(kk * kk, axis=-1, keepdims=True), 1e-24))
    a_h = heads(a)
    k_h = heads(k.astype(f32) * (1.0 + (a - 1.0) * k_a))
    r_h = heads(r.astype(f32))
    v_h = heads(v.astype(f32))
    d_h = heads(decay)

    def step(S, inp):
        r_t, d_t, k_t, v_t, a_t, b_t = inp
        sa = jnp.einsum('bhij,bhj->bhi', S, a_t)
        S = S * d_t[:, :, None, :] + sa[..., None] * b_t[:, :, None, :] + v_t[..., None] * k_t[:, :, None, :]
        return S, jnp.einsum('bhij,bhj->bhi', S, r_t)

    seqs = tuple(jnp.moveaxis(z, 1, 0) for z in (r_h, d_h, k_h, v_h, -kk, kk * a_h))
    s_fin, y = lax.scan(step, s0.astype(f32), seqs)
    y = jnp.moveaxis(y, 0, 1)
    mean = jnp.mean(y, axis=-1, keepdims=True)
    var = jnp.mean(jnp.square(y - mean), axis=-1, keepdims=True)
    y = ((y - mean) * lax.rsqrt(var + GN_EPS)).reshape(bsz, t, B_WIDTH) * ln_w + ln_b
    bonus = jnp.sum(r_h * k_h * r_k, axis=-1, keepdims=True) * v_h
    y = (y + bonus.reshape(bsz, t, B_WIDTH)) * g
    return y.astype(p.dtype), s_fin.astype(s0.dtype), p[:, -1:]


def hier_moe(h, w_rg, b_rg, w_re, b_re, w_gate, w_up, w_down):
    t = h.shape[0]
    f32 = jnp.float32
    grp_logits = (h @ w_rg).astype(f32) + b_rg.astype(f32)
    grp = jnp.argmax(grp_logits, axis=-1)
    p_grp = jnp.take_along_axis(jax.nn.softmax(grp_logits, axis=-1), grp[:, None], axis=-1)
    exp_logits = ((h @ w_re).astype(f32) + b_re.astype(f32)).reshape(t, N_GROUPS, EXPERTS_PER_GROUP)
    in_grp = jnp.take_along_axis(exp_logits, grp[:, None, None], axis=1)[:, 0]
    top_vals, top_idx = lax.top_k(in_grp, TOP_K)
    wts = jax.nn.softmax(top_vals, axis=-1) * p_grp
    ids = grp[:, None] * EXPERTS_PER_GROUP + top_idx
    gates = jnp.einsum('tk,tke->te', wts, jax.nn.one_hot(ids, N_EXPERTS, dtype=f32))
    hg = jnp.einsum('td,edf->tef', h, w_gate)
    hu = jnp.einsum('td,edf->tef', h, w_up)
    act = jax.nn.silu(hg) * hu * gates[:, :, None].astype(h.dtype)
    return jnp.einsum('tef,efd->td', act, w_down)


def setup_inputs(seed: int = 0) -> dict:
    key = jax.random.key(seed)
    ks = iter(jax.random.split(key, 64))
    f32 = jnp.float32

    def nrm(shape, scale):
        return jax.random.normal(next(ks), shape, f32) * scale

    def gain(shape):
        return 1.0 + nrm(shape, 0.02)

    def unif(shape, lo, hi):
        return jax.random.uniform(next(ks), shape, f32, lo, hi)

    n_pages = PAST_LEN // PAGE_SIZE
    n_used = DEC_BATCH * n_pages
    n_phys = n_used + (n_used + 3) // 4
    ds = D_MODEL ** -0.5
    out = {}
    out['x_prompt'] = nrm((BATCH, SEQ, D_MODEL), 1.0)
    out['x_sample'] = nrm((DEC_BATCH, DEC_SEQ, D_MODEL), 1.0)
    out['cache_a_k'] = nrm((n_phys, PAGE_SIZE, A_HEADS, A_QK), 1.0)
    out['cache_a_v'] = nrm((n_phys, PAGE_SIZE, A_HEADS, A_V), 1.0)
    out['state_rwkv'] = nrm((DEC_BATCH, B_HEADS, B_HEAD, B_HEAD), 0.1)
    out['state_shift'] = nrm((DEC_BATCH, 1, B_PROJ), 1.0)
    out['cache_sb_k'] = nrm((n_phys, PAGE_SIZE, C_HEADS, C_HEAD_DIM), 1.0)
    out['cache_sb_v'] = nrm((n_phys, PAGE_SIZE, C_HEADS, C_HEAD_DIM), 1.0)
    out['page_table'] = jax.random.permutation(next(ks), n_phys)[:n_used].reshape(DEC_BATCH, n_pages).astype(jnp.int32)
    out['norm_mix'] = gain((DEPTH, D_MODEL))
    out['norm_ffn'] = gain((DEPTH, D_MODEL))
    out['norm_final'] = gain((D_MODEL,))
    out['w_in0'] = nrm((D_MODEL, IN0_COLS), ds)
    out['lam_q1'] = nrm((A_HEAD_DIM,), 0.1)
    out['lam_k1'] = nrm((A_HEAD_DIM,), 0.1)
    out['lam_q2'] = nrm((A_HEAD_DIM,), 0.1)
    out['lam_k2'] = nrm((A_HEAD_DIM,), 0.1)
    out['subln0'] = gain((A_V,))
    out['rw_mu'] = unif((B_PROJ,), 0.0, 1.0)
    out['rw_w0'] = unif((B_WIDTH,), -6.0, -1.0)
    out['rw_w2'] = nrm((B_DECAY_LORA, B_WIDTH), 0.1)
    out['rw_a0'] = nrm((B_WIDTH,), 0.1)
    out['rw_a2'] = nrm((B_A_LORA, B_WIDTH), 0.1)
    out['rw_g2'] = nrm((B_G_LORA, B_WIDTH), B_G_LORA ** -0.5)
    out['rw_k_k'] = 0.85 + nrm((B_WIDTH,), 0.05)
    out['rw_k_a'] = 1.0 + nrm((B_WIDTH,), 0.05)
    out['rw_r_k'] = nrm((B_HEADS, B_HEAD), 0.1)
    out['rw_ln_w'] = gain((B_WIDTH,))
    out['rw_ln_b'] = nrm((B_WIDTH,), 0.02)
    out['w_out0'] = nrm((D_MODEL, D_MODEL), ds)
    out['w_qkv1'] = nrm((D_MODEL, 3 * D_MODEL), ds)
    out['w_out1'] = nrm((D_MODEL, D_MODEL), ds)
    out['router_grp_w'] = nrm((DEPTH, D_MODEL, N_GROUPS), ds)
    out['router_grp_b'] = nrm((DEPTH, N_GROUPS), 0.01)
    out['router_exp_w'] = nrm((DEPTH, D_MODEL, N_EXPERTS), ds)
    out['router_exp_b'] = nrm((DEPTH, N_EXPERTS), 0.01)
    out['exp_w_gate'] = nrm((DEPTH, N_EXPERTS, D_MODEL, EXPERT_HIDDEN), ds)
    out['exp_w_up'] = nrm((DEPTH, N_EXPERTS, D_MODEL, EXPERT_HIDDEN), ds)
    out['exp_w_down'] = nrm((DEPTH, N_EXPERTS, EXPERT_HIDDEN, D_MODEL), EXPERT_HIDDEN ** -0.5)
    return out


def reference(x_prompt, x_sample, cache_a_k, cache_a_v, state_rwkv, state_shift, cache_sb_k, cache_sb_v, page_table,
              norm_mix, norm_ffn, norm_final, w_in0, lam_q1, lam_k1, lam_q2, lam_k2, subln0,
              rw_mu, rw_w0, rw_w2, rw_a0, rw_a2, rw_g2, rw_k_k, rw_k_a, rw_r_k, rw_ln_w, rw_ln_b, w_out0,
              w_qkv1, w_out1, router_grp_w, router_grp_b, router_exp_w, router_exp_b,
              exp_w_gate, exp_w_up, exp_w_down):

    def mixer_ab(h, attend, s0, p_prev, lam, lam_init):
        bsz, t = h.shape[0], h.shape[1]
        proj = h @ w_in0
        qa = proj[..., :A_WIDTH].reshape(bsz, t, A_HEADS, A_QK)
        ka = proj[..., A_WIDTH:2 * A_WIDTH].reshape(bsz, t, A_HEADS, A_QK)
        va = proj[..., 2 * A_WIDTH:3 * A_WIDTH].reshape(bsz, t, A_HEADS, A_V)
        oa = attend(qa, ka, va, lam)
        oa = (rmsnorm(oa, subln0) * (1.0 - lam_init)).reshape(bsz, t, A_WIDTH)
        ob, s_new, shift_new = rwkv7_time_mix(proj[..., 3 * A_WIDTH:], p_prev, s0, rw_mu, rw_w0, rw_w2,
                                              rw_a0, rw_a2, rw_g2, rw_k_k, rw_k_a, rw_r_k, rw_ln_w, rw_ln_b)
        return jnp.concatenate([oa, ob], axis=-1) @ w_out0, ka, va, s_new, shift_new

    def mixer_c(h, attend):
        bsz, t = h.shape[0], h.shape[1]
        qkv = (h @ w_qkv1).reshape(bsz, t, 3, C_HEADS, C_HEAD_DIM)
        q, k, v = qkv[:, :, 0], qkv[:, :, 1], qkv[:, :, 2]
        return attend(q, k, v).reshape(bsz, t, D_MODEL) @ w_out1, k, v

    def attend_a_prompt(q, k, v, lam):
        k_pos = jnp.arange(k.shape[1], dtype=jnp.int32)
        return sweep_query_blocks(
            lambda qb, start: diff_attn_core(qb, k, v, start + jnp.arange(Q_BLOCK, dtype=jnp.int32), k_pos, lam), q)

    def attend_a_sample(q, k, v, lam):
        return paged_attend(lambda qq, kk, vv, qp, kp: diff_attn_core(qq, kk, vv, qp, kp, lam),
                            q, k, v, cache_a_k, cache_a_v, page_table)

    def attend_c_prompt(q, k, v):
        k_pos = jnp.arange(k.shape[1], dtype=jnp.int32)
        return sweep_query_blocks(
            lambda qb, start: stick_breaking_core(qb, k, v, start + jnp.arange(Q_BLOCK, dtype=jnp.int32), k_pos), q)

    def attend_c_sample(q, k, v):
        return paged_attend(stick_breaking_core, q, k, v, cache_sb_k, cache_sb_v, page_table)

    def channel_mix(x, layer, per_sequence):
        h = rmsnorm(x, norm_ffn[layer])

        def moe(z):
            return hier_moe(z, router_grp_w[layer], router_grp_b[layer], router_exp_w[layer], router_exp_b[layer],
                            exp_w_gate[layer], exp_w_up[layer], exp_w_down[layer])

        if per_sequence:
            return lax.map(moe, h)
        return moe(h.reshape(-1, D_MODEL)).reshape(h.shape)

    xp, xs = x_prompt, x_sample
    for layer in range(DEPTH):
        hp = rmsnorm(xp, norm_mix[layer])
        hs = rmsnorm(xs, norm_mix[layer])
        if layer % 2 == 0:
            lam_init = 0.8 - 0.6 * math.exp(-0.3 * layer)
            lam = (jnp.exp(jnp.sum(lam_q1.astype(jnp.float32) * lam_k1.astype(jnp.float32)))
                   - jnp.exp(jnp.sum(lam_q2.astype(jnp.float32) * lam_k2.astype(jnp.float32))) + lam_init)
            s0p = jnp.zeros((xp.shape[0], B_HEADS, B_HEAD, B_HEAD), state_rwkv.dtype)
            sh0p = jnp.zeros((xp.shape[0], 1, B_PROJ), xp.dtype)
            op, a_k_p, a_v_p, rwkv_p, shift_p = mixer_ab(hp, attend_a_prompt, s0p, sh0p, lam, lam_init)
            os_, a_k_s, a_v_s, rwkv_s, shift_s = mixer_ab(hs, attend_a_sample, state_rwkv, state_shift, lam, lam_init)
        else:
            op, sb_k_p, sb_v_p = mixer_c(hp, attend_c_prompt)
            os_, sb_k_s, sb_v_s = mixer_c(hs, attend_c_sample)
        xp = xp + op
        xs = xs + os_
        xp = xp + channel_mix(xp, layer, True)
        xs = xs + channel_mix(xs, layer, False)

    y_prompt = rmsnorm(xp, norm_final)
    y_sample = rmsnorm(xs, norm_final)
    return (y_prompt, y_sample, a_k_p, a_v_p, a_k_s, a_v_s, rwkv_p, rwkv_s, shift_p, shift_s,
            sb_k_p, sb_v_p, sb_k_s, sb_v_s)
```

```python
import functools
import math

import jax
import jax.numpy as jnp
from jax import lax
from jax.experimental import pallas as pl
from jax.experimental.pallas import tpu as pltpu

F32 = jnp.float32
BF16 = jnp.bfloat16
HIGHEST = lax.Precision.HIGHEST

D_MODEL = 2048
A_WIDTH = 1024
A_HEAD_DIM = 64
A_HEADS = 8
A_QK = 128
B_WIDTH = 1024
B_HEAD = 64
B_HEADS = 16
B_DECAY_LORA = 64
B_A_LORA = 64
B_G_LORA = 160
B_PROJ = 3 * B_WIDTH + B_DECAY_LORA + B_A_LORA + B_G_LORA
B_PROJ_PAD = 3456
B_TAIL = B_PROJ_PAD - 3 * B_WIDTH
C_HEADS = 16
C_HEAD_DIM = 128
N_GROUPS = 4
EXPERTS_PER_GROUP = 8
N_EXPERTS = 32
TOP_K = 2
EXPERT_HIDDEN = 1024
PAGE_SIZE = 128
RMS_EPS = 1e-6
GN_EPS = 64e-5
NEG_BIG = -1e30

LANES = 128
SUBLANES = 8
V7X_VMEM_BYTES = 64 * 1024 * 1024
VMEM_LIMIT = 48 * 1024 * 1024


def _params(sem, vmem=VMEM_LIMIT):
    return pltpu.CompilerParams(dimension_semantics=sem, vmem_limit_bytes=vmem)


def _dot(a, b):
    return jnp.dot(a, b, preferred_element_type=F32)


def _dot_nt(a, b):
    return lax.dot_general(a, b, (((1,), (1,)), ((), ())), preferred_element_type=F32)


def _dot_f32(a, b):
    return jnp.dot(a, b, preferred_element_type=F32, precision=HIGHEST)


def _norm_matmul_kernel(x_ref, g_ref, w_ref, *refs, n_out):
    outs, h_scr = refs[:n_out], refs[n_out]

    @pl.when(pl.program_id(1) == 0)
    def _():
        x = x_ref[...]
        y = x * lax.rsqrt(jnp.mean(x * x, axis=-1, keepdims=True) + RMS_EPS)
        h_scr[...] = (y * g_ref[...]).astype(BF16)

    acc = _dot(h_scr[...], w_ref[...])
    for o in outs:
        o[...] = acc.astype(o.dtype)


def norm_matmul(x, gain, w, out_dtypes, *, tm, tn):
    m, k = x.shape
    n = w.shape[1]
    assert m % tm == 0 and n % tn == 0
    outs = tuple(jax.ShapeDtypeStruct((m, n), dt) for dt in out_dtypes)
    return pl.pallas_call(
        functools.partial(_norm_matmul_kernel, n_out=len(outs)),
        out_shape=outs,
        grid=(m // tm, n // tn),
        in_specs=[pl.BlockSpec((tm, k), lambda i, j: (i, 0)),
                  pl.BlockSpec((1, k), lambda i, j: (0, 0)),
                  pl.BlockSpec((k, tn), lambda i, j: (0, j))],
        out_specs=tuple(pl.BlockSpec((tm, tn), lambda i, j: (i, j)) for _ in outs),
        scratch_shapes=[pltpu.VMEM((tm, k), BF16)],
        compiler_params=_params(("parallel", "arbitrary")),
        name="norm_matmul",
    )(x, gain.reshape(1, k), w)


def _matmul_res_kernel(*refs, n_a):
    a_refs, w_refs, r_ref, o_ref = refs[:n_a], refs[n_a:2 * n_a], refs[2 * n_a], refs[2 * n_a + 1]
    acc = r_ref[...]
    for a, w in zip(a_refs, w_refs):
        acc = acc + _dot(a[...], w[...])
    o_ref[...] = acc


def matmul_residual(a_list, w, resid, *, tm, tn):
    m, n = resid.shape
    n_a = len(a_list)
    kk = a_list[0].shape[1]
    assert all(a.shape == (m, kk) for a in a_list) and w.shape == (n_a * kk, n)
    in_specs = [pl.BlockSpec((tm, kk), lambda i, j: (i, 0)) for _ in a_list]
    in_specs += [pl.BlockSpec((kk, tn), lambda i, j, c=c: (c, j)) for c in range(n_a)]
    in_specs += [pl.BlockSpec((tm, tn), lambda i, j: (i, j))]
    return pl.pallas_call(
        functools.partial(_matmul_res_kernel, n_a=n_a),
        out_shape=jax.ShapeDtypeStruct((m, n), F32),
        grid=(m // tm, n // tn),
        in_specs=in_specs,
        out_specs=pl.BlockSpec((tm, tn), lambda i, j: (i, j)),
        compiler_params=_params(("parallel", "parallel")),
        name="matmul_residual",
    )(*a_list, *([w] * n_a), resid)


def _rmsnorm_kernel(x_ref, g_ref, o_ref):
    x = x_ref[...]
    y = x * lax.rsqrt(jnp.mean(x * x, axis=-1, keepdims=True) + RMS_EPS)
    o_ref[...] = y * g_ref[...]


def rmsnorm_rows(x, gain, *, tm):
    m, k = x.shape
    return pl.pallas_call(
        _rmsnorm_kernel,
        out_shape=jax.ShapeDtypeStruct((m, k), F32),
        grid=(m // tm,),
        in_specs=[pl.BlockSpec((tm, k), lambda i: (i, 0)), pl.BlockSpec((1, k), lambda i: (0, 0))],
        out_specs=pl.BlockSpec((tm, k), lambda i: (i, 0)),
        compiler_params=_params(("parallel",)),
        name="final_rmsnorm",
    )(x, gain.reshape(1, k))


def _lam_value(lq1, lk1, lq2, lk2, lam_init):
    s1 = jnp.sum(lq1[...] * lk1[...], axis=-1, keepdims=True)
    s2 = jnp.sum(lq2[...] * lk2[...], axis=-1, keepdims=True)
    return jnp.exp(s1) - jnp.exp(s2) + lam_init


def _subln(o, sub_ref, lam_init):
    y = o * lax.rsqrt(jnp.mean(o * o, axis=-1, keepdims=True) + RMS_EPS)
    return (y * sub_ref[...]) * (1.0 - lam_init)


def _softmax_step(s, m, l, acc, v):
    m_new = jnp.maximum(m, jnp.max(s, axis=-1, keepdims=True))
    alpha = jnp.exp(m - m_new)
    p = jnp.exp(s - m_new)
    l = alpha * l + jnp.sum(p, axis=-1, keepdims=True)
    acc = alpha * acc + _dot(p.astype(BF16), v)
    return m_new, l, acc


def _diff_attn_prompt_kernel(q_ref, k_ref, v_ref, lq1, lk1, lq2, lk2, sub_ref, o_ref, *, tq, lam_init):
    h = pl.program_id(1)
    i = pl.program_id(2)
    scale = A_HEAD_DIM ** -0.5
    q = q_ref[...]
    lane = lax.broadcasted_iota(jnp.int32, (tq, A_QK), 1)
    q1 = jnp.where(lane < A_HEAD_DIM, q, jnp.zeros_like(q))
    q2 = jnp.where(lane >= A_HEAD_DIM, q, jnp.zeros_like(q))
    hh = (h + 1).astype(F32) * (8.0 / A_HEADS)
    slope = jnp.exp2(jnp.zeros((1, tq), F32) - hh)
    qpos = (i * tq).astype(F32) + lax.broadcasted_iota(jnp.int32, (tq, 1), 0).astype(F32)
    kiota = lax.broadcasted_iota(jnp.int32, (1, tq), 1).astype(F32)

    def tile(j, carry, masked):
        m1, l1, a1, m2, l2, a2 = carry
        start = pl.multiple_of(j * tq, tq)
        kj = k_ref[pl.ds(start, tq), :]
        vj = v_ref[pl.ds(start, tq), :]
        dist = qpos - ((j * tq).astype(F32) + kiota)
        bias = slope * dist
        s1 = _dot_nt(q1, kj) * scale - bias
        s2 = _dot_nt(q2, kj) * scale - bias
        if masked:
            ok = dist >= 0.0
            s1 = jnp.where(ok, s1, NEG_BIG)
            s2 = jnp.where(ok, s2, NEG_BIG)
        m1, l1, a1 = _softmax_step(s1, m1, l1, a1, vj)
        m2, l2, a2 = _softmax_step(s2, m2, l2, a2, vj)
        return m1, l1, a1, m2, l2, a2

    z1 = jnp.zeros((tq, 1), F32)
    za = jnp.zeros((tq, A_QK), F32)
    init = (z1 + NEG_BIG, z1, za, z1 + NEG_BIG, z1, za)
    carry = lax.fori_loop(0, i, lambda j, c: tile(j, c, False), init)
    m1, l1, a1, m2, l2, a2 = tile(i, carry, True)
    lam = _lam_value(lq1, lk1, lq2, lk2, lam_init)
    o = a1 / l1 - lam * (a2 / l2)
    o_ref[...] = _subln(o, sub_ref, lam_init).astype(o_ref.dtype)


def diff_attn_prompt(q, k, v, lam_rows, subln, *, batch, seq, lam_init, tq=256):
    nq = seq // tq
    small = [pl.BlockSpec((1, A_HEAD_DIM), lambda b, h, i: (0, 0)) for _ in range(4)]
    return pl.pallas_call(
        functools.partial(_diff_attn_prompt_kernel, tq=tq, lam_init=lam_init),
        out_shape=jax.ShapeDtypeStruct((batch * seq, A_WIDTH), BF16),
        grid=(batch, A_HEADS, nq),
        in_specs=[pl.BlockSpec((tq, A_QK), lambda b, h, i: (b * nq + i, h)),
                  pl.BlockSpec((seq, A_QK), lambda b, h, i: (b, h)),
                  pl.BlockSpec((seq, A_QK), lambda b, h, i: (b, h))] + small
                 + [pl.BlockSpec((1, A_QK), lambda b, h, i: (0, 0))],
        out_specs=pl.BlockSpec((tq, A_QK), lambda b, h, i: (b * nq + i, h)),
        compiler_params=_params(("parallel", "parallel", "arbitrary")),
        name="diff_attn_prompt",
    )(q, k, v, *lam_rows, subln.reshape(1, A_QK))


def _diff_attn_sample_kernel(pt_ref, q_ref, kn_ref, vn_ref, kc_ref, vc_ref, lq1, lk1, lq2, lk2, sub_ref,
                             o_ref, qb_scr, m_scr, l_scr, acc_scr, *, n_pages, lam_init):
    p = pl.program_id(1)
    rows = 2 * A_HEADS
    scale = A_HEAD_DIM ** -0.5

    @pl.when(p == 0)
    def _():
        row = lax.broadcasted_iota(jnp.int32, (rows, A_WIDTH), 0)
        lane = lax.broadcasted_iota(jnp.int32, (rows, A_WIDTH), 1)
        keep = ((row % A_HEADS) == (lane // A_QK)) & (((lane % A_QK) < A_HEAD_DIM) == (row < A_HEADS))
        qb_scr[...] = jnp.where(keep, jnp.broadcast_to(q_ref[0], (rows, A_WIDTH)), 0.0)
        m_scr[...] = jnp.full((rows, 1), NEG_BIG, F32)
        l_scr[...] = jnp.zeros((rows, 1), F32)
        acc_scr[...] = jnp.zeros((rows, A_WIDTH), F32)

    head1 = (lax.broadcasted_iota(jnp.int32, (rows, 1), 0) % A_HEADS + 1).astype(F32) * (8.0 / A_HEADS)
    slope = jnp.exp2(-head1)
    kpos = (p * PAGE_SIZE).astype(F32) + lax.broadcasted_iota(jnp.int32, (1, PAGE_SIZE), 1).astype(F32)
    dist = float(n_pages * PAGE_SIZE) - kpos
    kp = kc_ref[0].astype(BF16)
    vp = vc_ref[0].astype(BF16)
    s = _dot_nt(qb_scr[...].astype(BF16), kp) * scale - slope * dist
    m, l, acc = _softmax_step(s, m_scr[...], l_scr[...], acc_scr[...], vp)
    m_scr[...] = m
    l_scr[...] = l
    acc_scr[...] = acc

    @pl.when(p == n_pages - 1)
    def _():
        s_new = jnp.sum(qb_scr[...] * kn_ref[0], axis=-1, keepdims=True) * scale
        m0, l0, acc0 = m_scr[...], l_scr[...], acc_scr[...]
        m_new = jnp.maximum(m0, s_new)
        alpha = jnp.exp(m0 - m_new)
        p_new = jnp.exp(s_new - m_new)
        l1 = alpha * l0 + p_new
        acc1 = alpha * acc0 + p_new * vn_ref[0]
        o16 = acc1 / l1
        lam = _lam_value(lq1, lk1, lq2, lk2, lam_init)
        od = o16[:A_HEADS] - lam * o16[A_HEADS:]
        rowh = lax.broadcasted_iota(jnp.int32, (A_HEADS, A_QK), 0)
        o = jnp.zeros((A_HEADS, A_QK), F32)
        for hh in range(A_HEADS):
            o = o + jnp.where(rowh == hh, od[:, hh * A_QK:(hh + 1) * A_QK], 0.0)
        o_ref[0] = _subln(o, sub_ref, lam_init).astype(o_ref.dtype)


def diff_attn_sample(q, k_new, v_new, cache_k, cache_v, page_table, lam_rows, subln, *, lam_init):
    s_n, n_pages = page_table.shape
    small = [pl.BlockSpec((1, A_HEAD_DIM), lambda s, p, pt: (0, 0)) for _ in range(4)]
    row_spec = pl.BlockSpec((1, 1, A_WIDTH), lambda s, p, pt: (s, 0, 0))
    page_spec = pl.BlockSpec((1, PAGE_SIZE, A_WIDTH), lambda s, p, pt: (pt[s, p], 0, 0))
    rows = 2 * A_HEADS
    out = pl.pallas_call(
        functools.partial(_diff_attn_sample_kernel, n_pages=n_pages, lam_init=lam_init),
        out_shape=jax.ShapeDtypeStruct((s_n, A_HEADS, A_QK), BF16),
        grid_spec=pltpu.PrefetchScalarGridSpec(
            num_scalar_prefetch=1,
            grid=(s_n, n_pages),
            in_specs=[row_spec, row_spec, row_spec, page_spec, page_spec] + small
                     + [pl.BlockSpec((1, A_QK), lambda s, p, pt: (0, 0))],
            out_specs=pl.BlockSpec((1, A_HEADS, A_QK), lambda s, p, pt: (s, 0, 0)),
            scratch_shapes=[pltpu.VMEM((rows, A_WIDTH), F32), pltpu.VMEM((rows, 1), F32),
                            pltpu.VMEM((rows, 1), F32), pltpu.VMEM((rows, A_WIDTH), F32)]),
        compiler_params=_params(("parallel", "arbitrary")),
        name="diff_attn_sample",
    )(page_table, q.reshape(s_n, 1, A_WIDTH), k_new.reshape(s_n, 1, A_WIDTH), v_new.reshape(s_n, 1, A_WIDTH),
      cache_k, cache_v, *lam_rows, subln.reshape(1, A_QK))
    return out.reshape(s_n, A_WIDTH)


def _log_sigmoid(z):
    return jnp.minimum(z, 0.0) - jnp.log1p(jnp.exp(-jnp.abs(z)))


def _suffix_sum(lk, tri):
    hi = lk.astype(BF16)
    lo = (lk - hi.astype(F32)).astype(BF16)
    return _dot(hi, tri) + _dot(lo, tri)


def _strict_lower_ones(n):
    r = lax.broadcasted_iota(jnp.int32, (n, n), 0)
    c = lax.broadcasted_iota(jnp.int32, (n, n), 1)
    return jnp.where(r > c, 1.0, 0.0).astype(BF16)


def _sb_prompt_kernel(q_ref, k_ref, v_ref, o_ref, *, tq):
    i = pl.program_id(2)
    scale = C_HEAD_DIM ** -0.5
    q = q_ref[...]
    tri = _strict_lower_ones(tq)
    rr = lax.broadcasted_iota(jnp.int32, (tq, tq), 0)
    cc = lax.broadcasted_iota(jnp.int32, (tq, tq), 1)
    before = cc < rr

    def tile(j, carry, masked):
        c, acc = carry
        start = pl.multiple_of(j * tq, tq)
        kj = k_ref[pl.ds(start, tq), :]
        vj = v_ref[pl.ds(start, tq), :]
        z = _dot_nt(q, kj) * scale
        ls = _log_sigmoid(z)
        lk = ls - z
        if masked:
            lk = jnp.where(before, lk, 0.0)
        later = c + _suffix_sum(lk, tri)
        att = jnp.exp(ls + later)
        if masked:
            att = jnp.where(before, att, 0.0)
        acc = acc + _dot(att.astype(BF16), vj)
        c = c + jnp.sum(lk, axis=-1, keepdims=True)
        return c, acc

    carry = tile(i, (jnp.zeros((tq, 1), F32), jnp.zeros((tq, C_HEAD_DIM), F32)), True)
    _, acc = lax.fori_loop(0, i, lambda jj, c: tile(i - 1 - jj, c, False), carry)
    o_ref[...] = acc.astype(o_ref.dtype)


def sb_attn_prompt(q, k, v, *, batch, seq, tq=256):
    nq = seq // tq
    width = C_HEADS * C_HEAD_DIM
    return pl.pallas_call(
        functools.partial(_sb_prompt_kernel, tq=tq),
        out_shape=jax.ShapeDtypeStruct((batch * seq, width), BF16),
        grid=(batch, C_HEADS, nq),
        in_specs=[pl.BlockSpec((tq, C_HEAD_DIM), lambda b, h, i: (b * nq + i, h)),
                  pl.BlockSpec((seq, C_HEAD_DIM), lambda b, h, i: (b, h)),
                  pl.BlockSpec((seq, C_HEAD_DIM), lambda b, h, i: (b, h))],
        out_specs=pl.BlockSpec((tq, C_HEAD_DIM), lambda b, h, i: (b * nq + i, h)),
        compiler_params=_params(("parallel", "parallel", "arbitrary")),
        name="sb_attn_prompt",
    )(q, k, v)


def _sb_sample_kernel(pt_ref, q_ref, kc_ref, vc_ref, o_ref, qb_scr, c_scr, acc_scr):
    p = pl.program_id(1)
    width = C_HEADS * C_HEAD_DIM
    scale = C_HEAD_DIM ** -0.5

    @pl.when(p == 0)
    def _():
        row = lax.broadcasted_iota(jnp.int32, (C_HEADS, width), 0)
        lane = lax.broadcasted_iota(jnp.int32, (C_HEADS, width), 1)
        qb_scr[...] = jnp.where(row == lane // C_HEAD_DIM, jnp.broadcast_to(q_ref[0], (C_HEADS, width)), 0.0)
        c_scr[...] = jnp.zeros((C_HEADS, 1), F32)
        acc_scr[...] = jnp.zeros((C_HEADS, width), F32)

    kp = kc_ref[0].astype(BF16)
    vp = vc_ref[0].astype(BF16)
    z = _dot_nt(qb_scr[...].astype(BF16), kp) * scale
    ls = _log_sigmoid(z)
    lk = ls - z
    c = c_scr[...]
    later = c + _suffix_sum(lk, _strict_lower_ones(PAGE_SIZE))
    att = jnp.exp(ls + later)
    acc = acc_scr[...] + _dot(att.astype(BF16), vp)
    acc_scr[...] = acc
    c_scr[...] = c + jnp.sum(lk, axis=-1, keepdims=True)

    @pl.when(p == pl.num_programs(1) - 1)
    def _():
        rowh = lax.broadcasted_iota(jnp.int32, (C_HEADS, C_HEAD_DIM), 0)
        o = jnp.zeros((C_HEADS, C_HEAD_DIM), F32)
        for hh in range(C_HEADS):
            o = o + jnp.where(rowh == hh, acc[:, hh * C_HEAD_DIM:(hh + 1) * C_HEAD_DIM], 0.0)
        o_ref[0] = o.astype(o_ref.dtype)


def sb_attn_sample(q, cache_k, cache_v, page_table):
    s_n, n_pages = page_table.shape
    width = C_HEADS * C_HEAD_DIM
    page_spec = pl.BlockSpec((1, PAGE_SIZE, width), lambda s, p, pt: (pt[s, n_pages - 1 - p], 0, 0))
    out = pl.pallas_call(
        _sb_sample_kernel,
        out_shape=jax.ShapeDtypeStruct((s_n, C_HEADS, C_HEAD_DIM), BF16),
        grid_spec=pltpu.PrefetchScalarGridSpec(
            num_scalar_prefetch=1,
            grid=(s_n, n_pages),
            in_specs=[pl.BlockSpec((1, 1, width), lambda s, p, pt: (s, 0, 0)), page_spec, page_spec],
            out_specs=pl.BlockSpec((1, C_HEADS, C_HEAD_DIM), lambda s, p, pt: (s, 0, 0)),
            scratch_shapes=[pltpu.VMEM((C_HEADS, width), F32), pltpu.VMEM((C_HEADS, 1), F32),
                            pltpu.VMEM((C_HEADS, width), F32)]),
        compiler_params=_params(("parallel", "arbitrary")),
        name="sb_attn_sample",
    )(page_table, q.reshape(s_n, 1, width), cache_k, cache_v)
    return out.reshape(s_n, width)


def _head_sum(x):
    r = lax.broadcasted_iota(jnp.int32, (LANES, LANES), 0) // B_HEAD
    c = lax.broadcasted_iota(jnp.int32, (LANES, LANES), 1) // B_HEAD
    ones = jnp.where(r == c, 1.0, 0.0).astype(F32)
    parts = [_dot_f32(x[:, g * LANES:(g + 1) * LANES], ones) for g in range(x.shape[1] // LANES)]
    return jnp.concatenate(parts, axis=1)


def _softplus(x):
    return jnp.maximum(x, 0.0) + jnp.log1p(jnp.exp(-jnp.abs(x)))


def _sigmoid(x):
    return 1.0 / (1.0 + jnp.exp(-x))


def _rwkv_prep_kernel(p_ref, prev_ref, mu_ref, w0_ref, a0_ref, kk_ref, ka_ref, rk_ref, w2_ref, a2_ref, g2_ref,
                      *refs, sequential, transpose):
    outs = refs[:8]
    p = p_ref[...]
    tm = p.shape[0]
    if sequential:
        last_scr = refs[8]
        t = pl.program_id(1)

        @pl.when(t == 0)
        def _():
            last_scr[...] = prev_ref[0]

        rolled = pltpu.roll(p, 1, axis=0)
        row = lax.broadcasted_iota(jnp.int32, (tm, 1), 0)
        shifted = jnp.where(row == 0, last_scr[...], rolled)
        last_scr[...] = p[tm - 1:tm, :]
    else:
        shifted = prev_ref[...]
    xs = p + (shifted - p) * mu_ref[...]
    r = xs[:, :B_WIDTH]
    k = xs[:, B_WIDTH:2 * B_WIDTH]
    v = xs[:, 2 * B_WIDTH:3 * B_WIDTH]
    tail = xs[:, 3 * B_WIDTH:]
    w = -_softplus(-(w0_ref[...] + _dot_f32(jnp.tanh(tail), w2_ref[...]))) - 0.5
    decay = jnp.exp(-jnp.exp(w))
    a = _sigmoid(a0_ref[...] + _dot_f32(tail, a2_ref[...]))
    g = _dot_f32(_sigmoid(tail), g2_ref[...])
    kk = k * kk_ref[...]
    kk = kk * lax.rsqrt(jnp.maximum(_head_sum(kk * kk), 1e-24))
    k_h = k * (1.0 + (a - 1.0) * ka_ref[...])
    bonus = _head_sum(r * k_h * rk_ref[...]) * v
    tr = (lambda z: z.T) if transpose else (lambda z: z)
    r_o, d_o, k_o, na_o, b_o, v_o, g_o, bonus_o = outs
    r_o[0] = tr(r)
    d_o[0] = tr(decay)
    k_o[0] = tr(k_h)
    na_o[0] = tr(-kk)
    b_o[0] = tr(kk * a)
    v_o[0] = v
    g_o[0] = g
    bonus_o[0] = bonus


def rwkv_prep(p, prev, weights, *, batch, seq, tm, sequential):
    nt = seq // tm if sequential else 1
    nb = batch if sequential else (batch * seq) // tm
    rows = batch * seq
    vec = lambda n: pl.BlockSpec((1, n), lambda b, t: (0, 0))
    mat = lambda: pl.BlockSpec((B_TAIL, B_WIDTH), lambda b, t: (0, 0))
    if sequential:
        p_spec = pl.BlockSpec((tm, B_PROJ_PAD), lambda b, t: (b * nt + t, 0))
        prev_spec = pl.BlockSpec((1, 1, B_PROJ_PAD), lambda b, t: (b, 0, 0))
        t_shape = jax.ShapeDtypeStruct((batch, B_WIDTH, seq), F32)
        t_spec = pl.BlockSpec((1, B_WIDTH, tm), lambda b, t: (b, 0, t))
        n_shape = jax.ShapeDtypeStruct((batch, seq, B_WIDTH), F32)
        n_spec = pl.BlockSpec((1, tm, B_WIDTH), lambda b, t: (b, t, 0))
        scratch = [pltpu.VMEM((1, B_PROJ_PAD), F32)]
    else:
        p_spec = pl.BlockSpec((tm, B_PROJ_PAD), lambda b, t: (b, 0))
        prev_spec = pl.BlockSpec((tm, B_PROJ_PAD), lambda b, t: (b, 0))
        t_shape = n_shape = jax.ShapeDtypeStruct((nb, tm, B_WIDTH), F32)
        t_spec = n_spec = pl.BlockSpec((1, tm, B_WIDTH), lambda b, t: (b, 0, 0))
        scratch = []
    del rows
    return pl.pallas_call(
        functools.partial(_rwkv_prep_kernel, sequential=sequential, transpose=sequential),
        out_shape=(t_shape,) * 5 + (n_shape,) * 3,
        grid=(nb, nt),
        in_specs=[p_spec, prev_spec, vec(B_PROJ_PAD), vec(B_WIDTH), vec(B_WIDTH), vec(B_WIDTH), vec(B_WIDTH),
                  vec(B_WIDTH), mat(), mat(), mat()],
        out_specs=(t_spec,) * 5 + (n_spec,) * 3,
        scratch_shapes=scratch,
        compiler_params=_params(("parallel", "arbitrary")),
        name="rwkv_prep",
    )(p, prev, *weights)


def _rwkv_scan_kernel(a_ref, r_ref, d_ref, b_ref, k_ref, v_ref, y_ref, s_ref, st_scr, col_scr, *, hg, tc, sub):
    t_idx = pl.program_id(2)

    @pl.when(t_idx == 0)
    def _():
        st_scr[...] = jnp.zeros(st_scr.shape, F32)

    op_refs = (a_ref, d_ref, b_ref, k_ref, r_ref)

    def sub_block(q, states):
        shift = (tc - q * sub) % tc
        for n, ref in enumerate(op_refs):
            col_scr[n] = pltpu.roll(ref[0], shift, axis=1)
        states = list(states)
        for c8 in range(sub // SUBLANES):
            t0 = pl.multiple_of(q * sub + c8 * SUBLANES, SUBLANES)
            v8 = v_ref[0, pl.ds(t0, SUBLANES), :]
            y8 = []
            for cc in range(SUBLANES):
                c = c8 * SUBLANES + cc
                y_row = []
                for h in range(hg):
                    rows = slice(h * B_HEAD, (h + 1) * B_HEAD)
                    a_c = col_scr[0, rows, c:c + 1]
                    d_c = col_scr[1, rows, c:c + 1]
                    b_c = col_scr[2, rows, c:c + 1]
                    k_c = col_scr[3, rows, c:c + 1]
                    r_c = col_scr[4, rows, c:c + 1]
                    v_row = v8[cc:cc + 1, h * B_HEAD:(h + 1) * B_HEAD]
                    st = states[h]
                    sa = jnp.sum(st * a_c, axis=0, keepdims=True)
                    st = st * d_c + b_c * sa + k_c * v_row
                    states[h] = st
                    y_row.append(jnp.sum(st * r_c, axis=0, keepdims=True))
                y8.append(jnp.concatenate(y_row, axis=1))
            y_ref[0, pl.ds(t0, SUBLANES), :] = jnp.concatenate(y8, axis=0)
        return tuple(states)

    init = tuple(st_scr[h] for h in range(hg))
    final = lax.fori_loop(0, tc // sub, sub_block, init)
    for h in range(hg):
        st_scr[h] = final[h]

    @pl.when(t_idx == pl.num_programs(2) - 1)
    def _():
        for h in range(hg):
            s_ref[0, h] = final[h]


def rwkv_scan(na_t, r_t, d_t, b_t, k_t, v, *, hg=4, tc=128, sub=32):
    batch, _, seq = r_t.shape
    wl = hg * B_HEAD
    t_spec = pl.BlockSpec((1, wl, tc), lambda b, g, t: (b, g, t))
    n_spec = pl.BlockSpec((1, tc, wl), lambda b, g, t: (b, t, g))
    return pl.pallas_call(
        functools.partial(_rwkv_scan_kernel, hg=hg, tc=tc, sub=sub),
        out_shape=(jax.ShapeDtypeStruct((batch, seq, B_WIDTH), F32),
                   jax.ShapeDtypeStruct((batch, B_HEADS, B_HEAD, B_HEAD), F32)),
        grid=(batch, B_HEADS // hg, seq // tc),
        in_specs=[t_spec] * 5 + [n_spec],
        out_specs=(n_spec, pl.BlockSpec((1, hg, B_HEAD, B_HEAD), lambda b, g, t: (b, g, 0, 0))),
        scratch_shapes=[pltpu.VMEM((hg, B_HEAD, B_HEAD), F32), pltpu.VMEM((5, wl, tc), F32)],
        compiler_params=_params(("parallel", "parallel", "arbitrary")),
        name="rwkv_scan",
    )(na_t, r_t, d_t, b_t, k_t, v)


def _rwkv_step_kernel(a_ref, r_ref, d_ref, b_ref, k_ref, v_ref, s0_ref, y_ref, s_ref):
    r_i = lax.broadcasted_iota(jnp.int32, (B_HEAD, B_HEAD), 0)
    c_i = lax.broadcasted_iota(jnp.int32, (B_HEAD, B_HEAD), 1)
    eye = jnp.where(r_i == c_i, 1.0, 0.0).astype(F32)
    for h in range(B_HEADS):
        row = lambda ref: ref[0, h:h + 1, :]
        st = s0_ref[0, h]
        sa = jnp.sum(st * row(a_ref), axis=1, keepdims=True)
        v_col = jnp.sum(eye * row(v_ref), axis=1, keepdims=True)
        st = st * row(d_ref) + sa * row(b_ref) + v_col * row(k_ref)
        s_ref[0, h] = st
        y_col = jnp.sum(st * row(r_ref), axis=1, keepdims=True)
        y_ref[0, h:h + 1, :] = jnp.sum(eye * y_col, axis=0, keepdims=True)


def rwkv_step(na, r, d, b, k, v, s0):
    s_n = s0.shape[0]
    hs = lambda z: z.reshape(s_n, B_HEADS, B_HEAD)
    vec = pl.BlockSpec((1, B_HEADS, B_HEAD), lambda s: (s, 0, 0))
    st = pl.BlockSpec((1, B_HEADS, B_HEAD, B_HEAD), lambda s: (s, 0, 0, 0))
    y, s_new = pl.pallas_call(
        _rwkv_step_kernel,
        out_shape=(jax.ShapeDtypeStruct((s_n, B_HEADS, B_HEAD), F32),
                   jax.ShapeDtypeStruct(s0.shape, F32)),
        grid=(s_n,),
        in_specs=[vec] * 6 + [st],
        out_specs=(vec, st),
        compiler_params=_params(("parallel",)),
        name="rwkv_step",
    )(hs(na), hs(r), hs(d), hs(b), hs(k), hs(v), s0)
    return y.reshape(s_n, B_WIDTH), s_new


def _rwkv_post_kernel(y_ref, bonus_ref, g_ref, lw_ref, lb_ref, o_ref):
    y = y_ref[...]
    mean = _head_sum(y) * (1.0 / B_HEAD)
    yc = y - mean
    var = _head_sum(yc * yc) * (1.0 / B_HEAD)
    out = yc * lax.rsqrt(var + GN_EPS) * lw_ref[...] + lb_ref[...]
    o_ref[...] = ((out + bonus_ref[...]) * g_ref[...]).astype(o_ref.dtype)


def rwkv_post(y, bonus, g, ln_w, ln_b, *, tm):
    m = y.shape[0]
    row = pl.BlockSpec((tm, B_WIDTH), lambda i: (i, 0))
    vec = pl.BlockSpec((1, B_WIDTH), lambda i: (0, 0))
    return pl.pallas_call(
        _rwkv_post_kernel,
        out_shape=jax.ShapeDtypeStruct((m, B_WIDTH), BF16),
        grid=(m // tm,),
        in_specs=[row, row, row, vec, vec],
        out_specs=row,
        compiler_params=_params(("parallel",)),
        name="rwkv_post",
    )(y, bonus, g, ln_w.reshape(1, B_WIDTH), ln_b.reshape(1, B_WIDTH))


def _router_kernel(x_ref, g_ref, w_ref, b_ref, h_ref, id_ref, wt_ref):
    x = x_ref[...]
    tm = x.shape[0]
    h = (x * lax.rsqrt(jnp.mean(x * x, axis=-1, keepdims=True) + RMS_EPS)) * g_ref[...]
    h_ref[...] = h.astype(BF16)
    logits = _dot_f32(h, w_ref[...]) + b_ref[...]
    lane = lax.broadcasted_iota(jnp.int32, (tm, LANES), 1).astype(F32)
    none = float(LANES)
    is_grp = lane < N_GROUPS
    g_max = jnp.max(jnp.where(is_grp, logits, NEG_BIG), axis=-1, keepdims=True)
    grp = jnp.min(jnp.where(is_grp & (logits == g_max), lane, none), axis=-1, keepdims=True)
    p_grp = 1.0 / jnp.sum(jnp.where(is_grp, jnp.exp(logits - g_max), 0.0), axis=-1, keepdims=True)
    lo = N_GROUPS + EXPERTS_PER_GROUP * grp
    in_grp = (lane >= lo) & (lane < lo + EXPERTS_PER_GROUP)
    v1 = jnp.max(jnp.where(in_grp, logits, NEG_BIG), axis=-1, keepdims=True)
    i1 = jnp.min(jnp.where(in_grp & (logits == v1), lane, none), axis=-1, keepdims=True)
    rest = in_grp & (lane != i1)
    v2 = jnp.max(jnp.where(rest, logits, NEG_BIG), axis=-1, keepdims=True)
    i2 = jnp.min(jnp.where(rest & (logits == v2), lane, none), axis=-1, keepdims=True)
    e = jnp.exp(v2 - v1)
    w1 = p_grp / (1.0 + e)
    w2 = p_grp * e / (1.0 + e)
    ids = jnp.where(lane == 0.0, i1 - N_GROUPS, jnp.where(lane == 1.0, i2 - N_GROUPS, 0.0))
    id_ref[...] = ids.astype(jnp.int32)
    wt_ref[...] = jnp.where(lane == 0.0, w1, jnp.where(lane == 1.0, w2, 0.0))


def moe_router(x, gain, w_router, b_router, *, tm):
    m, k = x.shape
    return pl.pallas_call(
        _router_kernel,
        out_shape=(jax.ShapeDtypeStruct((m, k), BF16), jax.ShapeDtypeStruct((m, LANES), jnp.int32),
                   jax.ShapeDtypeStruct((m, LANES), F32)),
        grid=(m // tm,),
        in_specs=[pl.BlockSpec((tm, k), lambda i: (i, 0)), pl.BlockSpec((1, k), lambda i: (0, 0)),
                  pl.BlockSpec((k, LANES), lambda i: (0, 0)), pl.BlockSpec((1, LANES), lambda i: (0, 0))],
        out_specs=(pl.BlockSpec((tm, k), lambda i: (i, 0)), pl.BlockSpec((tm, LANES), lambda i: (i, 0)),
                   pl.BlockSpec((tm, LANES), lambda i: (i, 0))),
        compiler_params=_params(("parallel",)),
        name="moe_router",
    )(x, gain.reshape(1, k), w_router, b_router)


def _expert_kernel(te_ref, nu_ref, x_ref, wg_ref, wu_ref, wd_ref, o_ref):
    i = pl.program_id(0)
    f = pl.program_id(1)
    used = i < nu_ref[0]

    @pl.when(used)
    def _():
        x = x_ref[...]
        hg = _dot(x, wg_ref[0, 0].astype(BF16))
        hu = _dot(x, wu_ref[0, 0].astype(BF16))
        act = (hg * _sigmoid(hg)) * hu
        part = _dot(act.astype(BF16), wd_ref[0, 0].astype(BF16))

        @pl.when(f == 0)
        def _():
            o_ref[...] = part

        @pl.when(f > 0)
        def _():
            o_ref[...] += part

    @pl.when(jnp.logical_not(used) & (f == 0))
    def _():
        o_ref[...] = jnp.zeros(o_ref.shape, F32)


def moe_experts(xs, tile_expert, n_used, w_gate, w_up, w_down, *, layer, tm, fc):
    rows, d = xs.shape
    n_tiles = rows // tm
    nf = EXPERT_HIDDEN // fc

    def live(i, f, te, nu):
        ok = i < nu[0]
        return jnp.where(ok, i, nu[0] - 1), jnp.where(ok, f, nf - 1)

    def x_map(i, f, te, nu):
        return live(i, f, te, nu)[0], 0

    def gu_map(i, f, te, nu):
        ii, ff = live(i, f, te, nu)
        return layer, te[ii], 0, ff

    def d_map(i, f, te, nu):
        ii, ff = live(i, f, te, nu)
        return layer, te[ii], ff, 0

    return pl.pallas_call(
        _expert_kernel,
        out_shape=jax.ShapeDtypeStruct((rows, d), F32),
        grid_spec=pltpu.PrefetchScalarGridSpec(
            num_scalar_prefetch=2,
            grid=(n_tiles, nf),
            in_specs=[pl.BlockSpec((tm, d), x_map),
                      pl.BlockSpec((1, 1, d, fc), gu_map),
                      pl.BlockSpec((1, 1, d, fc), gu_map),
                      pl.BlockSpec((1, 1, fc, d), d_map)],
            out_specs=pl.BlockSpec((tm, d), lambda i, f, te, nu: (i, 0))),
        compiler_params=_params(("arbitrary", "arbitrary"), vmem=56 * 1024 * 1024),
        name="moe_experts",
    )(tile_expert, n_used, xs, w_gate, w_up, w_down)


def _combine_kernel(x_ref, y1_ref, y2_ref, wt_ref, o_ref):
    wt = wt_ref[...]
    o_ref[...] = x_ref[...] + wt[:, 0:1] * y1_ref[...] + wt[:, 1:2] * y2_ref[...]


def moe_combine(x, y1, y2, wts, *, tm):
    m, k = x.shape
    row = pl.BlockSpec((tm, k), lambda i: (i, 0))
    return pl.pallas_call(
        _combine_kernel,
        out_shape=jax.ShapeDtypeStruct((m, k), F32),
        grid=(m // tm,),
        in_specs=[row, row, row, pl.BlockSpec((tm, LANES), lambda i: (i, 0))],
        out_specs=row,
        compiler_params=_params(("parallel",)),
        name="moe_combine",
    )(x, y1, y2, wts)


def _routing_tables(ids, *, tm):
    n_assign = ids.shape[0] * TOP_K
    n_tiles = (n_assign + N_EXPERTS * (tm - 1)) // tm
    flat = ids.reshape(-1)
    onehot = (flat[:, None] == jnp.arange(N_EXPERTS, dtype=jnp.int32)[None, :]).astype(jnp.int32)
    counts = jnp.sum(onehot, axis=0)
    rank = jnp.take_along_axis(jnp.cumsum(onehot, axis=0), flat[:, None], axis=1)[:, 0] - 1
    tiles_per = (counts + tm - 1) // tm
    tile_end = jnp.cumsum(tiles_per)
    row_start = (tile_end - tiles_per) * tm
    dest = row_start[flat] + rank
    n_used = tile_end[-1]
    tile_ids = jnp.arange(n_tiles, dtype=jnp.int32)
    tile_expert = jnp.minimum(jnp.searchsorted(tile_end, tile_ids, side="right"), N_EXPERTS - 1).astype(jnp.int32)
    last_used = tile_expert[jnp.maximum(n_used - 1, 0)]
    tile_expert = jnp.where(tile_ids < n_used, tile_expert, last_used)
    src = jnp.zeros((n_tiles * tm,), jnp.int32).at[dest].set(jnp.arange(n_assign, dtype=jnp.int32) // TOP_K)
    return src, dest.reshape(-1, TOP_K), tile_expert, n_used.reshape(1).astype(jnp.int32)


def hier_moe_layer(xp, xs, layer, norm_ffn, router_w, router_b, w_gate, w_up, w_down, *, tm_e=512, fc=256):
    hp, idp, wtp = moe_router(xp, norm_ffn[layer], router_w, router_b, tm=512)
    hs, ids_, wts_ = moe_router(xs, norm_ffn[layer], router_w, router_b, tm=xs.shape[0])
    h_all = jnp.concatenate([hp, hs], axis=0)
    ids = jnp.concatenate([idp[:, :TOP_K], ids_[:, :TOP_K]], axis=0)
    src, dest, tile_expert, n_used = _routing_tables(ids, tm=tm_e)
    x_sorted = jnp.take(h_all, src, axis=0)
    y = moe_experts(x_sorted, tile_expert, n_used, w_gate, w_up, w_down, layer=layer, tm=tm_e, fc=fc)
    y1 = jnp.take(y, dest[:, 0], axis=0)
    y2 = jnp.take(y, dest[:, 1], axis=0)
    n_p = xp.shape[0]
    xp = moe_combine(xp, y1[:n_p], y2[:n_p], wtp, tm=512)
    xs = moe_combine(xs, y1[n_p:], y2[n_p:], wts_, tm=xs.shape[0])
    return xp, xs


def _pad_cols(w, n):
    return jnp.pad(w, ((0, 0), (0, n - w.shape[1])))


def kernel(x_prompt, x_sample, cache_a_k, cache_a_v, state_rwkv, state_shift, cache_sb_k, cache_sb_v, page_table,
           norm_mix, norm_ffn, norm_final, w_in0, lam_q1, lam_k1, lam_q2, lam_k2, subln0,
           rw_mu, rw_w0, rw_w2, rw_a0, rw_a2, rw_g2, rw_k_k, rw_k_a, rw_r_k, rw_ln_w, rw_ln_b, w_out0,
           w_qkv1, w_out1, router_grp_w, router_grp_b, router_exp_w, router_exp_b,
           exp_w_gate, exp_w_up, exp_w_down):
    batch, seq, d = x_prompt.shape
    s_n = x_sample.shape[0]
    n_p = batch * seq
    xp = x_prompt.reshape(n_p, d)
    xs = x_sample.reshape(s_n, d)
    n_phys = cache_a_k.shape[0]
    tm_p = 512

    lam_init = 0.8 - 0.6 * math.exp(-0.3 * 0)
    lam_rows = [z.reshape(1, A_HEAD_DIM) for z in (lam_q1, lam_k1, lam_q2, lam_k2)]
    w_q = w_in0[:, :A_WIDTH].astype(BF16)
    w_k = w_in0[:, A_WIDTH:2 * A_WIDTH].astype(BF16)
    w_v = w_in0[:, 2 * A_WIDTH:3 * A_WIDTH].astype(BF16)
    w_p = _pad_cols(w_in0[:, 3 * A_WIDTH:], B_PROJ_PAD).astype(BF16)
    g0 = norm_mix[0]

    (qp16,) = norm_matmul(xp, g0, w_q, (BF16,), tm=tm_p, tn=512)
    kp32, kp16 = norm_matmul(xp, g0, w_k, (F32, BF16), tm=tm_p, tn=512)
    vp32, vp16 = norm_matmul(xp, g0, w_v, (F32, BF16), tm=tm_p, tn=512)
    (pp,) = norm_matmul(xp, g0, w_p, (F32,), tm=tm_p, tn=1152)
    (qs32,) = norm_matmul(xs, g0, w_q, (F32,), tm=s_n, tn=512)
    (ks32,) = norm_matmul(xs, g0, w_k, (F32,), tm=s_n, tn=512)
    (vs32,) = norm_matmul(xs, g0, w_v, (F32,), tm=s_n, tn=512)
    (ps,) = norm_matmul(xs, g0, w_p, (F32,), tm=s_n, tn=1152)

    oa_p = diff_attn_prompt(qp16, kp16, vp16, lam_rows, subln0, batch=batch, seq=seq, lam_init=lam_init)
    oa_s = diff_attn_sample(qs32, ks32, vs32, cache_a_k.reshape(n_phys, PAGE_SIZE, A_WIDTH),
                            cache_a_v.reshape(n_phys, PAGE_SIZE, A_WIDTH), page_table, lam_rows, subln0,
                            lam_init=lam_init)

    zeros_tail = jnp.zeros((B_TAIL - B_DECAY_LORA - B_A_LORA - B_G_LORA, B_WIDTH), F32)
    w2_pad = jnp.concatenate([rw_w2, jnp.zeros((B_A_LORA + B_G_LORA, B_WIDTH), F32), zeros_tail], axis=0)
    a2_pad = jnp.concatenate([jnp.zeros((B_DECAY_LORA, B_WIDTH), F32), rw_a2,
                              jnp.zeros((B_G_LORA, B_WIDTH), F32), zeros_tail], axis=0)
    g2_pad = jnp.concatenate([jnp.zeros((B_DECAY_LORA + B_A_LORA, B_WIDTH), F32), rw_g2, zeros_tail], axis=0)
    row = lambda z: z.reshape(1, -1)
    prep_w = (row(jnp.pad(rw_mu, (0, B_PROJ_PAD - B_PROJ))), row(rw_w0), row(rw_a0), row(rw_k_k), row(rw_k_a),
              row(rw_r_k), w2_pad, a2_pad, g2_pad)

    r_t, d_t, k_t, na_t, b_t, v_n, g_n, bonus_n = rwkv_prep(
        pp, jnp.zeros((batch, 1, B_PROJ_PAD), F32), prep_w, batch=batch, seq=seq, tm=256, sequential=True)
    y_p, st_p = rwkv_scan(na_t, r_t, d_t, b_t, k_t, v_n)
    ob_p = rwkv_post(y_p.reshape(n_p, B_WIDTH), bonus_n.reshape(n_p, B_WIDTH), g_n.reshape(n_p, B_WIDTH),
                     rw_ln_w, rw_ln_b, tm=tm_p)
    rwkv_p = jnp.swapaxes(st_p, 2, 3)

    prev_s = jnp.pad(state_shift.reshape(s_n, B_PROJ), ((0, 0), (0, B_PROJ_PAD - B_PROJ)))
    r_s, d_s, k_s, na_s, b_s, v_s, g_s, bonus_s = (z.reshape(s_n, B_WIDTH) for z in rwkv_prep(
        ps, prev_s, prep_w, batch=s_n, seq=1, tm=s_n, sequential=False))
    y_s, rwkv_s = rwkv_step(na_s, r_s, d_s, b_s, k_s, v_s, state_rwkv)
    ob_s = rwkv_post(y_s, bonus_s, g_s, rw_ln_w, rw_ln_b, tm=s_n)

    w_o0 = w_out0.astype(BF16)
    xp = matmul_residual([oa_p, ob_p], w_o0, xp, tm=tm_p, tn=512)
    xs = matmul_residual([oa_s, ob_s], w_o0, xs, tm=s_n, tn=512)

    shift_p = pp.reshape(batch, seq, B_PROJ_PAD)[:, seq - 1:, :B_PROJ]
    shift_s = ps[:, :B_PROJ].reshape(s_n, 1, B_PROJ)
    a_k_p = kp32.reshape(batch, seq, A_HEADS, A_QK)
    a_v_p = vp32.reshape(batch, seq, A_HEADS, A_QK)
    a_k_s = ks32.reshape(s_n, 1, A_HEADS, A_QK)
    a_v_s = vs32.reshape(s_n, 1, A_HEADS, A_QK)

    def router_weights(layer):
        w = jnp.concatenate([router_grp_w[layer], router_exp_w[layer]], axis=1)
        b = jnp.concatenate([router_grp_b[layer], router_exp_b[layer]], axis=0)
        return _pad_cols(w, LANES), jnp.pad(b, (0, LANES - b.shape[0])).reshape(1, LANES)

    rw0, rb0 = router_weights(0)
    xp, xs = hier_moe_layer(xp, xs, 0, norm_ffn, rw0, rb0, exp_w_gate, exp_w_up, exp_w_down)

    g1 = norm_mix[1]
    w_q1 = w_qkv1[:, :d].astype(BF16)
    w_k1 = w_qkv1[:, d:2 * d].astype(BF16)
    w_v1 = w_qkv1[:, 2 * d:].astype(BF16)
    (qp16,) = norm_matmul(xp, g1, w_q1, (BF16,), tm=tm_p, tn=512)
    kp32, kp16 = norm_matmul(xp, g1, w_k1, (F32, BF16), tm=tm_p, tn=512)
    vp32, vp16 = norm_matmul(xp, g1, w_v1, (F32, BF16), tm=tm_p, tn=512)
    (qs32,) = norm_matmul(xs, g1, w_q1, (F32,), tm=s_n, tn=512)
    (ks32,) = norm_matmul(xs, g1, w_k1, (F32,), tm=s_n, tn=512)
    (vs32,) = norm_matmul(xs, g1, w_v1, (F32,), tm=s_n, tn=512)

    oc_p = sb_attn_prompt(qp16, kp16, vp16, batch=batch, seq=seq)
    oc_s = sb_attn_sample(qs32, cache_sb_k.reshape(n_phys, PAGE_SIZE, d), cache_sb_v.reshape(n_phys, PAGE_SIZE, d),
                          page_table)
    w_o1 = w_out1.astype(BF16)
    xp = matmul_residual([oc_p], w_o1, xp, tm=tm_p, tn=512)
    xs = matmul_residual([oc_s], w_o1, xs, tm=s_n, tn=512)

    sb_k_p = kp32.reshape(batch, seq, C_HEADS, C_HEAD_DIM)
    sb_v_p = vp32.reshape(batch, seq, C_HEADS, C_HEAD_DIM)
    sb_k_s = ks32.reshape(s_n, 1, C_HEADS, C_HEAD_DIM)
    sb_v_s = vs32.reshape(s_n, 1, C_HEADS, C_HEAD_DIM)

    rw1, rb1 = router_weights(1)
    xp, xs = hier_moe_layer(xp, xs, 1, norm_ffn, rw1, rb1, exp_w_gate, exp_w_up, exp_w_down)

    y_prompt = rmsnorm_rows(xp, norm_final, tm=tm_p).reshape(batch, seq, d)
    y_sample = rmsnorm_rows(xs, norm_final, tm=s_n).reshape(s_n, 1, d)
    return (y_prompt, y_sample, a_k_p, a_v_p, a_k_s, a_v_s, rwkv_p, rwkv_s, shift_p, shift_s,
            sb_k_p, sb_v_p, sb_k_s, sb_v_s)
```

```python
import functools
import math

import jax
import jax.numpy as jnp
from jax import lax
from jax.experimental import pallas as pl
from jax.experimental.pallas import tpu as pltpu

F32 = jnp.float32
BF16 = jnp.bfloat16
HIGHEST = lax.Precision.HIGHEST

D_MODEL = 2048
A_WIDTH = 1024
A_HEAD_DIM = 64
A_HEADS = 8
A_QK = 128
B_WIDTH = 1024
B_HEAD = 64
B_HEADS = 16
B_DECAY_LORA = 64
B_A_LORA = 64
B_G_LORA = 160
B_PROJ = 3 * B_WIDTH + B_DECAY_LORA + B_A_LORA + B_G_LORA
B_PROJ_PAD = 3456
B_TAIL = B_PROJ_PAD - 3 * B_WIDTH
C_HEADS = 16
C_HEAD_DIM = 128
N_GROUPS = 4
EXPERTS_PER_GROUP = 8
N_EXPERTS = 32
TOP_K = 2
EXPERT_HIDDEN = 1024
PAGE_SIZE = 128
RMS_EPS = 1e-6
GN_EPS = 64e-5
NEG_BIG = -1e30

LANES = 128
SUBLANES = 8
V7X_VMEM_BYTES = 64 * 1024 * 1024
VMEM_LIMIT = 48 * 1024 * 1024


def _params(sem, vmem=VMEM_LIMIT):
    return pltpu.CompilerParams(dimension_semantics=sem, vmem_limit_bytes=vmem)


def _dot(a, b):
    return jnp.dot(a, b, preferred_element_type=F32)


def _dot_nt(a, b):
    return lax.dot_general(a, b, (((1,), (1,)), ((), ())), preferred_element_type=F32)


def _dot_f32(a, b):
    return jnp.dot(a, b, preferred_element_type=F32, precision=HIGHEST)


def _norm_matmul_kernel(x_ref, g_ref, w_ref, *refs, n_out):
    outs, h_scr = refs[:n_out], refs[n_out]

    @pl.when(pl.program_id(1) == 0)
    def _():
        x = x_ref[...]
        y = x * lax.rsqrt(jnp.mean(x * x, axis=-1, keepdims=True) + RMS_EPS)
        h_scr[...] = (y * g_ref[...]).astype(BF16)

    acc = _dot(h_scr[...], w_ref[...])
    for o in outs:
        o[...] = acc.astype(o.dtype)


def norm_matmul(x, gain, w, out_dtypes, *, tm, tn):
    m, k = x.shape
    n = w.shape[1]
    assert m % tm == 0 and n % tn == 0
    outs = tuple(jax.ShapeDtypeStruct((m, n), dt) for dt in out_dtypes)
    return pl.pallas_call(
        functools.partial(_norm_matmul_kernel, n_out=len(outs)),
        out_shape=outs,
        grid=(m // tm, n // tn),
        in_specs=[pl.BlockSpec((tm, k), lambda i, j: (i, 0)),
                  pl.BlockSpec((1, k), lambda i, j: (0, 0)),
                  pl.BlockSpec((k, tn), lambda i, j: (0, j))],
        out_specs=tuple(pl.BlockSpec((tm, tn), lambda i, j: (i, j)) for _ in outs),
        scratch_shapes=[pltpu.VMEM((tm, k), BF16)],
        compiler_params=_params(("parallel", "arbitrary")),
        name="norm_matmul",
    )(x, gain.reshape(1, k), w)


def _matmul_res_kernel(*refs, n_a):
    a_refs, w_refs, r_ref, o_ref = refs[:n_a], refs[n_a:2 * n_a], refs[2 * n_a], refs[2 * n_a + 1]
    acc = r_ref[...]
    for a, w in zip(a_refs, w_refs):
        acc = acc + _dot(a[...], w[...])
    o_ref[...] = acc


def matmul_residual(a_list, w, resid, *, tm, tn):
    m, n = resid.shape
    n_a = len(a_list)
    kk = a_list[0].shape[1]
    assert all(a.shape == (m, kk) for a in a_list) and w.shape == (n_a * kk, n)
    in_specs = [pl.BlockSpec((tm, kk), lambda i, j: (i, 0)) for _ in a_list]
    in_specs += [pl.BlockSpec((kk, tn), lambda i, j, c=c: (c, j)) for c in range(n_a)]
    in_specs += [pl.BlockSpec((tm, tn), lambda i, j: (i, j))]
    return pl.pallas_call(
        functools.partial(_matmul_res_kernel, n_a=n_a),
        out_shape=jax.ShapeDtypeStruct((m, n), F32),
        grid=(m // tm, n // tn),
        in_specs=in_specs,
        out_specs=pl.BlockSpec((tm, tn), lambda i, j: (i, j)),
        compiler_params=_params(("parallel", "parallel")),
        name="matmul_residual",
    )(*a_list, *([w] * n_a), resid)


def _rmsnorm_kernel(x_ref, g_ref, o_ref):
    x = x_ref[...]
    y = x * lax.rsqrt(jnp.mean(x * x, axis=-1, keepdims=True) + RMS_EPS)
    o_ref[...] = y * g_ref[...]


def rmsnorm_rows(x, gain, *, tm):
    m, k = x.shape
    return pl.pallas_call(
        _rmsnorm_kernel,
        out_shape=jax.ShapeDtypeStruct((m, k), F32),
        grid=(m // tm,),
        in_specs=[pl.BlockSpec((tm, k), lambda i: (i, 0)), pl.BlockSpec((1, k), lambda i: (0, 0))],
        out_specs=pl.BlockSpec((tm, k), lambda i: (i, 0)),
        compiler_params=_params(("parallel",)),
        name="final_rmsnorm",
    )(x, gain.reshape(1, k))


def _lam_value(lq1, lk1, lq2, lk2, lam_init):
    s1 = jnp.sum(lq1[...] * lk1[...], axis=-1, keepdims=True)
    s2 = jnp.sum(lq2[...] * lk2[...], axis=-1, keepdims=True)
    return jnp.exp(s1) - jnp.exp(s2) + lam_init


def _subln(o, sub_ref, lam_init):
    y = o * lax.rsqrt(jnp.mean(o * o, axis=-1, keepdims=True) + RMS_EPS)
    return (y * sub_ref[...]) * (1.0 - lam_init)


def _softmax_step(s, m, l, acc, v):
    m_new = jnp.maximum(m, jnp.max(s, axis=-1, keepdims=True))
    alpha = jnp.exp(m - m_new)
    p = jnp.exp(s - m_new)
    l = alpha * l + jnp.sum(p, axis=-1, keepdims=True)
    acc = alpha * acc + _dot(p.astype(BF16), v)
    return m_new, l, acc


def _diff_attn_prompt_kernel(q_ref, k_ref, v_ref, lq1, lk1, lq2, lk2, sub_ref, o_ref, *, tq, lam_init):
    h = pl.program_id(1)
    i = pl.program_id(2)
    scale = A_HEAD_DIM ** -0.5
    q = q_ref[...]
    lane = lax.broadcasted_iota(jnp.int32, (tq, A_QK), 1)
    q1 = jnp.where(lane < A_HEAD_DIM, q, jnp.zeros_like(q))
    q2 = jnp.where(lane >= A_HEAD_DIM, q, jnp.zeros_like(q))
    hh = (h + 1).astype(F32) * (8.0 / A_HEADS)
    slope = jnp.exp2(jnp.zeros((1, tq), F32) - hh)
    qpos = (i * tq).astype(F32) + lax.broadcasted_iota(jnp.int32, (tq, 1), 0).astype(F32)
    kiota = lax.broadcasted_iota(jnp.int32, (1, tq), 1).astype(F32)

    def tile(j, carry, masked):
        m1, l1, a1, m2, l2, a2 = carry
        start = pl.multiple_of(j * tq, tq)
        kj = k_ref[pl.ds(start, tq), :]
        vj = v_ref[pl.ds(start, tq), :]
        dist = qpos - ((j * tq).astype(F32) + kiota)
        bias = slope * dist
        s1 = _dot_nt(q1, kj) * scale - bias
        s2 = _dot_nt(q2, kj) * scale - bias
        if masked:
            ok = dist >= 0.0
            s1 = jnp.where(ok, s1, NEG_BIG)
            s2 = jnp.where(ok, s2, NEG_BIG)
        m1, l1, a1 = _softmax_step(s1, m1, l1, a1, vj)
        m2, l2, a2 = _softmax_step(s2, m2, l2, a2, vj)
        return m1, l1, a1, m2, l2, a2

    z1 = jnp.zeros((tq, 1), F32)
    za = jnp.zeros((tq, A_QK), F32)
    init = (z1 + NEG_BIG, z1, za, z1 + NEG_BIG, z1, za)
    carry = lax.fori_loop(0, i, lambda j, c: tile(j, c, False), init)
    m1, l1, a1, m2, l2, a2 = tile(i, carry, True)
    lam = _lam_value(lq1, lk1, lq2, lk2, lam_init)
    o = a1 / l1 - lam * (a2 / l2)
    o_ref[...] = _subln(o, sub_ref, lam_init).astype(o_ref.dtype)


def diff_attn_prompt(q, k, v, lam_rows, subln, *, batch, seq, lam_init, tq=256):
    nq = seq // tq
    small = [pl.BlockSpec((1, A_HEAD_DIM), lambda b, h, i: (0, 0)) for _ in range(4)]
    return pl.pallas_call(
        functools.partial(_diff_attn_prompt_kernel, tq=tq, lam_init=lam_init),
        out_shape=jax.ShapeDtypeStruct((batch * seq, A_WIDTH), BF16),
        grid=(batch, A_HEADS, nq),
        in_specs=[pl.BlockSpec((tq, A_QK), lambda b, h, i: (b * nq + i, h)),
                  pl.BlockSpec((seq, A_QK), lambda b, h, i: (b, h)),
                  pl.BlockSpec((seq, A_QK), lambda b, h, i: (b, h))] + small
                 + [pl.BlockSpec((1, A_QK), lambda b, h, i: (0, 0))],
        out_specs=pl.BlockSpec((tq, A_QK), lambda b, h, i: (b * nq + i, h)),
        compiler_params=_params(("parallel", "parallel", "arbitrary")),
        name="diff_attn_prompt",
    )(q, k, v, *lam_rows, subln.reshape(1, A_QK))


def _diff_attn_sample_kernel(pt_ref, q_ref, kn_ref, vn_ref, kc_ref, vc_ref, lq1, lk1, lq2, lk2, sub_ref,
                             o_ref, m_scr, l_scr, acc_scr, *, n_pages, lam_init):
    p = pl.program_id(1)
    rows = 2 * A_HEADS
    n_keys = PAGE_SIZE * A_HEADS
    scale = A_HEAD_DIM ** -0.5

    @pl.when(p == 0)
    def _():
        m_scr[...] = jnp.full((rows, 1), NEG_BIG, F32)
        l_scr[...] = jnp.zeros((rows, 1), F32)
        acc_scr[...] = jnp.zeros((rows, A_QK), F32)

    q = q_ref[0]
    lane = lax.broadcasted_iota(jnp.int32, (A_HEADS, A_QK), 1)
    qm = jnp.concatenate([jnp.where(lane < A_HEAD_DIM, q, 0.0), jnp.where(lane >= A_HEAD_DIM, q, 0.0)], axis=0)
    row = lax.broadcasted_iota(jnp.int32, (rows, n_keys), 0)
    col = lax.broadcasted_iota(jnp.int32, (rows, n_keys), 1)
    own = (col % A_HEADS) == (row % A_HEADS)
    head1 = (lax.broadcasted_iota(jnp.int32, (rows, 1), 0) % A_HEADS + 1).astype(F32) * (8.0 / A_HEADS)
    slope = jnp.exp2(-head1)
    kpos = (p * PAGE_SIZE).astype(F32) + (col // A_HEADS).astype(F32)
    dist = float(n_pages * PAGE_SIZE) - kpos
    kf = kc_ref[0].astype(BF16)
    vf = vc_ref[0].astype(BF16)
    s = _dot_nt(qm.astype(BF16), kf) * scale - slope * dist
    s = jnp.where(own, s, NEG_BIG)
    m, l, acc = _softmax_step(s, m_scr[...], l_scr[...], acc_scr[...], vf)
    m_scr[...] = m
    l_scr[...] = l
    acc_scr[...] = acc

    @pl.when(p == n_pages - 1)
    def _():
        kn2 = jnp.concatenate([kn_ref[0], kn_ref[0]], axis=0)
        vn2 = jnp.concatenate([vn_ref[0], vn_ref[0]], axis=0)
        s_new = jnp.sum(qm * kn2, axis=-1, keepdims=True) * scale
        m_new = jnp.maximum(m, s_new)
        alpha = jnp.exp(m - m_new)
        p_new = jnp.exp(s_new - m_new)
        o16 = (alpha * acc + p_new * vn2) / (alpha * l + p_new)
        lam = _lam_value(lq1, lk1, lq2, lk2, lam_init)
        o = o16[:A_HEADS] - lam * o16[A_HEADS:]
        o_ref[0] = _subln(o, sub_ref, lam_init).astype(o_ref.dtype)


def diff_attn_sample(q, k_new, v_new, cache_k, cache_v, page_table, lam_rows, subln, *, lam_init):
    s_n, n_pages = page_table.shape
    n_keys = PAGE_SIZE * A_HEADS
    small = [pl.BlockSpec((1, A_HEAD_DIM), lambda s, p, pt: (0, 0)) for _ in range(4)]
    row_spec = pl.BlockSpec((1, A_HEADS, A_QK), lambda s, p, pt: (s, 0, 0))
    page_spec = pl.BlockSpec((1, n_keys, A_QK), lambda s, p, pt: (pt[s, p], 0, 0))
    rows = 2 * A_HEADS
    heads = lambda z: z.reshape(s_n, A_HEADS, A_QK)
    out = pl.pallas_call(
        functools.partial(_diff_attn_sample_kernel, n_pages=n_pages, lam_init=lam_init),
        out_shape=jax.ShapeDtypeStruct((s_n, A_HEADS, A_QK), BF16),
        grid_spec=pltpu.PrefetchScalarGridSpec(
            num_scalar_prefetch=1,
            grid=(s_n, n_pages),
            in_specs=[row_spec, row_spec, row_spec, page_spec, page_spec] + small
                     + [pl.BlockSpec((1, A_QK), lambda s, p, pt: (0, 0))],
            out_specs=pl.BlockSpec((1, A_HEADS, A_QK), lambda s, p, pt: (s, 0, 0)),
            scratch_shapes=[pltpu.VMEM((rows, 1), F32), pltpu.VMEM((rows, 1), F32),
                            pltpu.VMEM((rows, A_QK), F32)]),
        compiler_params=_params(("parallel", "arbitrary")),
        name="diff_attn_sample",
    )(page_table, heads(q), heads(k_new), heads(v_new), cache_k, cache_v, *lam_rows, subln.reshape(1, A_QK))
    return out.reshape(s_n, A_WIDTH)


def _log_sigmoid(z):
    return jnp.minimum(z, 0.0) - jnp.log1p(jnp.exp(-jnp.abs(z)))


def _suffix_sum(lk, tri):
    hi = lk.astype(BF16)
    lo = (lk - hi.astype(F32)).astype(BF16)
    return _dot(hi, tri) + _dot(lo, tri)


def _strict_lower_ones(n):
    r = lax.broadcasted_iota(jnp.int32, (n, n), 0)
    c = lax.broadcasted_iota(jnp.int32, (n, n), 1)
    return jnp.where(r > c, 1.0, 0.0).astype(BF16)


def _sb_prompt_kernel(q_ref, k_ref, v_ref, o_ref, *, tq):
    i = pl.program_id(2)
    scale = C_HEAD_DIM ** -0.5
    q = q_ref[...]
    tri = _strict_lower_ones(tq)
    rr = lax.broadcasted_iota(jnp.int32, (tq, tq), 0)
    cc = lax.broadcasted_iota(jnp.int32, (tq, tq), 1)
    before = cc < rr

    def tile(j, carry, masked):
        c, acc = carry
        start = pl.multiple_of(j * tq, tq)
        kj = k_ref[pl.ds(start, tq), :]
        vj = v_ref[pl.ds(start, tq), :]
        z = _dot_nt(q, kj) * scale
        ls = _log_sigmoid(z)
        lk = ls - z
        if masked:
            lk = jnp.where(before, lk, 0.0)
        later = c + _suffix_sum(lk, tri)
        att = jnp.exp(ls + later)
        if masked:
            att = jnp.where(before, att, 0.0)
        acc = acc + _dot(att.astype(BF16), vj)
        c = c + jnp.sum(lk, axis=-1, keepdims=True)
        return c, acc

    carry = tile(i, (jnp.zeros((tq, 1), F32), jnp.zeros((tq, C_HEAD_DIM), F32)), True)
    _, acc = lax.fori_loop(0, i, lambda jj, c: tile(i - 1 - jj, c, False), carry)
    o_ref[...] = acc.astype(o_ref.dtype)


def sb_attn_prompt(q, k, v, *, batch, seq, tq=256):
    nq = seq // tq
    width = C_HEADS * C_HEAD_DIM
    return pl.pallas_call(
        functools.partial(_sb_prompt_kernel, tq=tq),
        out_shape=jax.ShapeDtypeStruct((batch * seq, width), BF16),
        grid=(batch, C_HEADS, nq),
        in_specs=[pl.BlockSpec((tq, C_HEAD_DIM), lambda b, h, i: (b * nq + i, h)),
                  pl.BlockSpec((seq, C_HEAD_DIM), lambda b, h, i: (b, h)),
                  pl.BlockSpec((seq, C_HEAD_DIM), lambda b, h, i: (b, h))],
        out_specs=pl.BlockSpec((tq, C_HEAD_DIM), lambda b, h, i: (b * nq + i, h)),
        compiler_params=_params(("parallel", "parallel", "arbitrary")),
        name="sb_attn_prompt",
    )(q, k, v)


def _sb_sample_kernel(pt_ref, q_ref, kc_ref, vc_ref, o_ref, c_scr, acc_scr):
    p = pl.program_id(1)
    n_tiles = PAGE_SIZE * C_HEADS // LANES
    scale = C_HEAD_DIM ** -0.5

    @pl.when(p == 0)
    def _():
        c_scr[...] = jnp.zeros((1, LANES), F32)
        acc_scr[...] = jnp.zeros((C_HEADS, C_HEAD_DIM), F32)

    kf = kc_ref[0].astype(BF16)
    vf = vc_ref[0].astype(BF16)
    zt = _dot_nt(q_ref[0].astype(BF16), kf)
    row = lax.broadcasted_iota(jnp.int32, (C_HEADS, LANES), 0)
    lane = lax.broadcasted_iota(jnp.int32, (C_HEADS, LANES), 1)
    own = (lane % C_HEADS) == row
    z = jnp.concatenate([jnp.sum(jnp.where(own, zt[:, g * LANES:(g + 1) * LANES], 0.0), axis=0, keepdims=True)
                         for g in range(n_tiles)], axis=0) * scale
    ls = _log_sigmoid(z)
    lk = ls - z
    li = lax.broadcasted_iota(jnp.int32, (LANES, LANES), 0)
    lj = lax.broadcasted_iota(jnp.int32, (LANES, LANES), 1)
    same = (li % C_HEADS) == (lj % C_HEADS)
    hi = lk.astype(BF16)
    lo = (lk - hi.astype(F32)).astype(BF16)
    sel = jnp.concatenate([jnp.where(same & (li > lj), 1.0, 0.0), jnp.where(same, 1.0, 0.0)], axis=1).astype(BF16)
    both = _dot(hi, sel) + _dot(lo, sel)
    inside, total = both[:, :LANES], both[:, LANES:]
    run = c_scr[...]
    later = [None] * n_tiles
    for g in range(n_tiles - 1, -1, -1):
        later[g] = run + inside[g:g + 1]
        run = run + total[g:g + 1]
    c_scr[...] = run
    att = jnp.exp(ls + jnp.concatenate(later, axis=0))
    spread = jnp.concatenate([jnp.where(own, att[g:g + 1], 0.0) for g in range(n_tiles)], axis=1)
    acc = acc_scr[...] + _dot(spread.astype(BF16), vf)
    acc_scr[...] = acc

    @pl.when(p == pl.num_programs(1) - 1)
    def _():
        o_ref[0] = acc.astype(o_ref.dtype)


def sb_attn_sample(q, cache_k, cache_v, page_table):
    s_n, n_pages = page_table.shape
    n_keys = PAGE_SIZE * C_HEADS
    page_spec = pl.BlockSpec((1, n_keys, C_HEAD_DIM), lambda s, p, pt: (pt[s, n_pages - 1 - p], 0, 0))
    out = pl.pallas_call(
        _sb_sample_kernel,
        out_shape=jax.ShapeDtypeStruct((s_n, C_HEADS, C_HEAD_DIM), BF16),
        grid_spec=pltpu.PrefetchScalarGridSpec(
            num_scalar_prefetch=1,
            grid=(s_n, n_pages),
            in_specs=[pl.BlockSpec((1, C_HEADS, C_HEAD_DIM), lambda s, p, pt: (s, 0, 0)), page_spec, page_spec],
            out_specs=pl.BlockSpec((1, C_HEADS, C_HEAD_DIM), lambda s, p, pt: (s, 0, 0)),
            scratch_shapes=[pltpu.VMEM((1, LANES), F32), pltpu.VMEM((C_HEADS, C_HEAD_DIM), F32)]),
        compiler_params=_params(("parallel", "arbitrary")),
        name="sb_attn_sample",
    )(page_table, q.reshape(s_n, C_HEADS, C_HEAD_DIM), cache_k, cache_v)
    return out.reshape(s_n, C_HEADS * C_HEAD_DIM)


def _head_sum(x):
    r = lax.broadcasted_iota(jnp.int32, (LANES, LANES), 0) // B_HEAD
    c = lax.broadcasted_iota(jnp.int32, (LANES, LANES), 1) // B_HEAD
    ones = jnp.where(r == c, 1.0, 0.0).astype(F32)
    parts = [_dot_f32(x[:, g * LANES:(g + 1) * LANES], ones) for g in range(x.shape[1] // LANES)]
    return jnp.concatenate(parts, axis=1)


def _softplus(x):
    return jnp.maximum(x, 0.0) + jnp.log1p(jnp.exp(-jnp.abs(x)))


def _sigmoid(x):
    return 1.0 / (1.0 + jnp.exp(-x))


def _rwkv_prep_kernel(p_ref, prev_ref, mu_ref, w0_ref, a0_ref, kk_ref, ka_ref, rk_ref, w2_ref, a2_ref, g2_ref,
                      *refs, sequential):
    outs = refs[:8]
    p = p_ref[...]
    tm = p.shape[0]
    if sequential:
        last_scr = refs[8]
        t = pl.program_id(1)

        @pl.when(t == 0)
        def _():
            last_scr[...] = prev_ref[0]

        rolled = pltpu.roll(p, 1, axis=0)
        row = lax.broadcasted_iota(jnp.int32, (tm, 1), 0)
        shifted = jnp.where(row == 0, last_scr[...], rolled)
        last_scr[...] = p[tm - 1:tm, :]
    else:
        shifted = prev_ref[...]
    xs = p + (shifted - p) * mu_ref[...]
    r = xs[:, :B_WIDTH]
    k = xs[:, B_WIDTH:2 * B_WIDTH]
    v = xs[:, 2 * B_WIDTH:3 * B_WIDTH]
    tail = xs[:, 3 * B_WIDTH:]
    w = -_softplus(-(w0_ref[...] + _dot_f32(jnp.tanh(tail), w2_ref[...]))) - 0.5
    decay = jnp.exp(-jnp.exp(w))
    a = _sigmoid(a0_ref[...] + _dot_f32(tail, a2_ref[...]))
    g = _dot_f32(_sigmoid(tail), g2_ref[...])
    kk = k * kk_ref[...]
    kk = kk * lax.rsqrt(jnp.maximum(_head_sum(kk * kk), 1e-24))
    k_h = k * (1.0 + (a - 1.0) * ka_ref[...])
    bonus = _head_sum(r * k_h * rk_ref[...]) * v
    r_o, d_o, k_o, na_o, b_o, v_o, g_o, bonus_o = outs
    r_o[0] = r
    d_o[0] = decay
    k_o[0] = k_h
    na_o[0] = -kk
    b_o[0] = kk * a
    v_o[0] = v
    g_o[0] = g
    bonus_o[0] = bonus


def rwkv_prep(p, prev, weights, *, batch, seq, tm, sequential):
    nt = seq // tm if sequential else 1
    nb = batch if sequential else (batch * seq) // tm
    vec = lambda n: pl.BlockSpec((1, n), lambda b, t: (0, 0))
    mat = lambda: pl.BlockSpec((B_TAIL, B_WIDTH), lambda b, t: (0, 0))
    if sequential:
        p_spec = pl.BlockSpec((tm, B_PROJ_PAD), lambda b, t: (b * nt + t, 0))
        prev_spec = pl.BlockSpec((1, 1, B_PROJ_PAD), lambda b, t: (b, 0, 0))
        n_shape = jax.ShapeDtypeStruct((batch, seq, B_WIDTH), F32)
        n_spec = pl.BlockSpec((1, tm, B_WIDTH), lambda b, t: (b, t, 0))
        scratch = [pltpu.VMEM((1, B_PROJ_PAD), F32)]
    else:
        p_spec = pl.BlockSpec((tm, B_PROJ_PAD), lambda b, t: (b, 0))
        prev_spec = pl.BlockSpec((tm, B_PROJ_PAD), lambda b, t: (b, 0))
        n_shape = jax.ShapeDtypeStruct((nb, tm, B_WIDTH), F32)
        n_spec = pl.BlockSpec((1, tm, B_WIDTH), lambda b, t: (b, 0, 0))
        scratch = []
    return pl.pallas_call(
        functools.partial(_rwkv_prep_kernel, sequential=sequential),
        out_shape=(n_shape,) * 8,
        grid=(nb, nt),
        in_specs=[p_spec, prev_spec, vec(B_PROJ_PAD), vec(B_WIDTH), vec(B_WIDTH), vec(B_WIDTH), vec(B_WIDTH),
                  vec(B_WIDTH), mat(), mat(), mat()],
        out_specs=(n_spec,) * 8,
        scratch_shapes=scratch,
        compiler_params=_params(("parallel", "arbitrary")),
        name="rwkv_prep",
    )(p, prev, *weights)


def _rwkv_scan_kernel(a_ref, r_ref, d_ref, b_ref, k_ref, v_ref, y_ref, s_ref, st_scr, *, nb, tc, n_groups):
    t_idx = pl.program_id(1)
    n_pairs = B_HEADS // 2

    @pl.when(t_idx == 0)
    def _():
        st_scr[...] = jnp.zeros(st_scr.shape, F32)

    r_i = lax.broadcasted_iota(jnp.int32, (B_HEAD, LANES), 0)
    l_i = lax.broadcasted_iota(jnp.int32, (B_HEAD, LANES), 1)
    diag = (l_i % B_HEAD) == r_i
    kr = lax.broadcasted_iota(jnp.int32, (LANES, LANES), 0) // B_HEAD
    kc = lax.broadcasted_iota(jnp.int32, (LANES, LANES), 1) // B_HEAD
    ones_blk = jnp.where(kr == kc, 1.0, 0.0).astype(BF16)
    chains = [(bb, pr) for bb in range(nb) for pr in range(n_pairs)]
    per = len(chains) // n_groups
    groups = [chains[g * per:(g + 1) * per] for g in range(n_groups)]
    refs = dict(a=a_ref, r=r_ref, d=d_ref, b=b_ref, k=k_ref, v=v_ref)

    def block8(blk, carry):
        t0 = pl.multiple_of(blk * SUBLANES, SUBLANES)
        rows8 = {n: [ref[bb, pl.ds(t0, SUBLANES), :] for bb in range(nb)] for n, ref in refs.items()}
        y_rows = [[[None] * n_pairs for _ in range(SUBLANES)] for _ in range(nb)]
        for cc in range(SUBLANES):
            for grp in groups:
                row = lambda n, bb, pr: rows8[n][bb][cc:cc + 1, pr * LANES:(pr + 1) * LANES]
                lhs = []
                for bb, pr in grp:
                    st = st_scr[bb * n_pairs + pr]
                    lhs.append((st * row("a", bb, pr)).astype(BF16))
                    lhs.append(jnp.where(diag, row("v", bb, pr), 0.0).astype(BF16))
                res = _dot(jnp.concatenate(lhs, axis=0), ones_blk)
                lhs = []
                for n, (bb, pr) in enumerate(grp):
                    sa_b = res[n * LANES:n * LANES + B_HEAD]
                    v_b = res[n * LANES + B_HEAD:(n + 1) * LANES]
                    st = st_scr[bb * n_pairs + pr]
                    st = st * row("d", bb, pr) + sa_b * row("b", bb, pr) + v_b * row("k", bb, pr)
                    st_scr[bb * n_pairs + pr] = st
                    lhs.append((st * row("r", bb, pr)).astype(BF16))
                res = _dot(jnp.concatenate(lhs, axis=0), ones_blk)
                for n, (bb, pr) in enumerate(grp):
                    y_b = res[n * B_HEAD:(n + 1) * B_HEAD]
                    y_rows[bb][cc][pr] = jnp.sum(jnp.where(diag, y_b, 0.0), axis=0, keepdims=True)
        for bb in range(nb):
            y8 = [jnp.concatenate(y_rows[bb][cc], axis=1) for cc in range(SUBLANES)]
            y_ref[bb, pl.ds(t0, SUBLANES), :] = jnp.concatenate(y8, axis=0)
        return carry

    lax.fori_loop(0, tc // SUBLANES, block8, 0)

    @pl.when(t_idx == pl.num_programs(1) - 1)
    def _():
        for bb, pr in chains:
            st = st_scr[bb * n_pairs + pr]
            for half in range(2):
                s_ref[bb, 2 * pr + half] = st[:, half * B_HEAD:(half + 1) * B_HEAD]


def rwkv_scan(na, r, d, b, k, v, *, nb=2, tc=128, n_groups=1):
    batch, seq, _ = r.shape
    spec = pl.BlockSpec((nb, tc, B_WIDTH), lambda g, t: (g, t, 0))
    return pl.pallas_call(
        functools.partial(_rwkv_scan_kernel, nb=nb, tc=tc, n_groups=n_groups),
        out_shape=(jax.ShapeDtypeStruct((batch, seq, B_WIDTH), F32),
                   jax.ShapeDtypeStruct((batch, B_HEADS, B_HEAD, B_HEAD), F32)),
        grid=(batch // nb, seq // tc),
        in_specs=[spec] * 6,
        out_specs=(spec, pl.BlockSpec((nb, B_HEADS, B_HEAD, B_HEAD), lambda g, t: (g, 0, 0, 0))),
        scratch_shapes=[pltpu.VMEM((nb * B_HEADS // 2, B_HEAD, LANES), F32)],
        compiler_params=_params(("parallel", "arbitrary")),
        name="rwkv_scan",
    )(na, r, d, b, k, v)


def _rwkv_step_kernel(a_ref, r_ref, d_ref, b_ref, k_ref, v_ref, s0_ref, y_ref, s_ref):
    r_i = lax.broadcasted_iota(jnp.int32, (B_HEAD, B_HEAD), 0)
    c_i = lax.broadcasted_iota(jnp.int32, (B_HEAD, B_HEAD), 1)
    eye = jnp.where(r_i == c_i, 1.0, 0.0).astype(F32)
    for h in range(B_HEADS):
        row = lambda ref: ref[0, h:h + 1, :]
        st = s0_ref[0, h]
        sa = jnp.sum(st * row(a_ref), axis=1, keepdims=True)
        v_col = jnp.sum(eye * row(v_ref), axis=1, keepdims=True)
        st = st * row(d_ref) + sa * row(b_ref) + v_col * row(k_ref)
        s_ref[0, h] = st
        y_col = jnp.sum(st * row(r_ref), axis=1, keepdims=True)
        y_ref[0, h:h + 1, :] = jnp.sum(eye * y_col, axis=0, keepdims=True)


def rwkv_step(na, r, d, b, k, v, s0):
    s_n = s0.shape[0]
    hs = lambda z: z.reshape(s_n, B_HEADS, B_HEAD)
    vec = pl.BlockSpec((1, B_HEADS, B_HEAD), lambda s: (s, 0, 0))
    st = pl.BlockSpec((1, B_HEADS, B_HEAD, B_HEAD), lambda s: (s, 0, 0, 0))
    y, s_new = pl.pallas_call(
        _rwkv_step_kernel,
        out_shape=(jax.ShapeDtypeStruct((s_n, B_HEADS, B_HEAD), F32),
                   jax.ShapeDtypeStruct(s0.shape, F32)),
        grid=(s_n,),
        in_specs=[vec] * 6 + [st],
        out_specs=(vec, st),
        compiler_params=_params(("parallel",)),
        name="rwkv_step",
    )(hs(na), hs(r), hs(d), hs(b), hs(k), hs(v), s0)
    return y.reshape(s_n, B_WIDTH), s_new


def _rwkv_post_kernel(y_ref, bonus_ref, g_ref, lw_ref, lb_ref, o_ref):
    y = y_ref[...]
    mean = _head_sum(y) * (1.0 / B_HEAD)
    yc = y - mean
    var = _head_sum(yc * yc) * (1.0 / B_HEAD)
    out = yc * lax.rsqrt(var + GN_EPS) * lw_ref[...] + lb_ref[...]
    o_ref[...] = ((out + bonus_ref[...]) * g_ref[...]).astype(o_ref.dtype)


def rwkv_post(y, bonus, g, ln_w, ln_b, *, tm):
    m = y.shape[0]
    row = pl.BlockSpec((tm, B_WIDTH), lambda i: (i, 0))
    vec = pl.BlockSpec((1, B_WIDTH), lambda i: (0, 0))
    return pl.pallas_call(
        _rwkv_post_kernel,
        out_shape=jax.ShapeDtypeStruct((m, B_WIDTH), BF16),
        grid=(m // tm,),
        in_specs=[row, row, row, vec, vec],
        out_specs=row,
        compiler_params=_params(("parallel",)),
        name="rwkv_post",
    )(y, bonus, g, ln_w.reshape(1, B_WIDTH), ln_b.reshape(1, B_WIDTH))


def _router_kernel(x_ref, g_ref, w_ref, b_ref, h_ref, id_ref, wt_ref):
    x = x_ref[...]
    tm = x.shape[0]
    h = (x * lax.rsqrt(jnp.mean(x * x, axis=-1, keepdims=True) + RMS_EPS)) * g_ref[...]
    h_ref[...] = h.astype(BF16)
    logits = _dot_f32(h, w_ref[...]) + b_ref[...]
    lane = lax.broadcasted_iota(jnp.int32, (tm, LANES), 1).astype(F32)
    none = float(LANES)
    is_grp = lane < N_GROUPS
    g_max = jnp.max(jnp.where(is_grp, logits, NEG_BIG), axis=-1, keepdims=True)
    grp = jnp.min(jnp.where(is_grp & (logits == g_max), lane, none), axis=-1, keepdims=True)
    p_grp = 1.0 / jnp.sum(jnp.where(is_grp, jnp.exp(logits - g_max), 0.0), axis=-1, keepdims=True)
    lo = N_GROUPS + EXPERTS_PER_GROUP * grp
    in_grp = (lane >= lo) & (lane < lo + EXPERTS_PER_GROUP)
    v1 = jnp.max(jnp.where(in_grp, logits, NEG_BIG), axis=-1, keepdims=True)
    i1 = jnp.min(jnp.where(in_grp & (logits == v1), lane, none), axis=-1, keepdims=True)
    rest = in_grp & (lane != i1)
    v2 = jnp.max(jnp.where(rest, logits, NEG_BIG), axis=-1, keepdims=True)
    i2 = jnp.min(jnp.where(rest & (logits == v2), lane, none), axis=-1, keepdims=True)
    e = jnp.exp(v2 - v1)
    w1 = p_grp / (1.0 + e)
    w2 = p_grp * e / (1.0 + e)
    ids = jnp.where(lane == 0.0, i1 - N_GROUPS, jnp.where(lane == 1.0, i2 - N_GROUPS, 0.0))
    id_ref[...] = ids.astype(jnp.int32)
    wt_ref[...] = jnp.where(lane == 0.0, w1, jnp.where(lane == 1.0, w2, 0.0))


def moe_router(x, gain, w_router, b_router, *, tm):
    m, k = x.shape
    return pl.pallas_call(
        _router_kernel,
        out_shape=(jax.ShapeDtypeStruct((m, k), BF16), jax.ShapeDtypeStruct((m, LANES), jnp.int32),
                   jax.ShapeDtypeStruct((m, LANES), F32)),
        grid=(m // tm,),
        in_specs=[pl.BlockSpec((tm, k), lambda i: (i, 0)), pl.BlockSpec((1, k), lambda i: (0, 0)),
                  pl.BlockSpec((k, LANES), lambda i: (0, 0)), pl.BlockSpec((1, LANES), lambda i: (0, 0))],
        out_specs=(pl.BlockSpec((tm, k), lambda i: (i, 0)), pl.BlockSpec((tm, LANES), lambda i: (i, 0)),
                   pl.BlockSpec((tm, LANES), lambda i: (i, 0))),
        compiler_params=_params(("parallel",)),
        name="moe_router",
    )(x, gain.reshape(1, k), w_router, b_router)


def _expert_kernel(te_ref, nu_ref, x_ref, wg_ref, wu_ref, wd_ref, o_ref):
    i = pl.program_id(0)
    f = pl.program_id(1)
    used = i < nu_ref[0]

    @pl.when(used)
    def _():
        x = x_ref[...]
        hg = _dot(x, wg_ref[0, 0].astype(BF16))
        hu = _dot(x, wu_ref[0, 0].astype(BF16))
        act = (hg * _sigmoid(hg)) * hu
        part = _dot(act.astype(BF16), wd_ref[0, 0].astype(BF16))

        @pl.when(f == 0)
        def _():
            o_ref[...] = part

        @pl.when(f > 0)
        def _():
            o_ref[...] += part

    @pl.when(jnp.logical_not(used) & (f == 0))
    def _():
        o_ref[...] = jnp.zeros(o_ref.shape, F32)


def moe_experts(xs, tile_expert, n_used, w_gate, w_up, w_down, *, layer, tm, fc):
    rows, d = xs.shape
    n_tiles = rows // tm
    nf = EXPERT_HIDDEN // fc

    def live(i, f, te, nu):
        ok = i < nu[0]
        return jnp.where(ok, i, nu[0] - 1), jnp.where(ok, f, nf - 1)

    def x_map(i, f, te, nu):
        return live(i, f, te, nu)[0], 0

    def gu_map(i, f, te, nu):
        ii, ff = live(i, f, te, nu)
        return layer, te[ii], 0, ff

    def d_map(i, f, te, nu):
        ii, ff = live(i, f, te, nu)
        return layer, te[ii], ff, 0

    return pl.pallas_call(
        _expert_kernel,
        out_shape=jax.ShapeDtypeStruct((rows, d), F32),
        grid_spec=pltpu.PrefetchScalarGridSpec(
            num_scalar_prefetch=2,
            grid=(n_tiles, nf),
            in_specs=[pl.BlockSpec((tm, d), x_map),
                      pl.BlockSpec((1, 1, d, fc), gu_map),
                      pl.BlockSpec((1, 1, d, fc), gu_map),
                      pl.BlockSpec((1, 1, fc, d), d_map)],
            out_specs=pl.BlockSpec((tm, d), lambda i, f, te, nu: (i, 0))),
        compiler_params=_params(("arbitrary", "arbitrary"), vmem=56 * 1024 * 1024),
        name="moe_experts",
    )(tile_expert, n_used, xs, w_gate, w_up, w_down)


def _combine_kernel(dest_ref, x_ref, wt_ref, y_hbm, o_ref, buf, sem, *, tm, n_tok, tok0):
    i = pl.program_id(0)

    def row_copy(k, choice):
        src_row = dest_ref[choice * n_tok + tok0 + i * tm + k]
        return pltpu.make_async_copy(y_hbm.at[pl.ds(src_row, 1)], buf.at[choice, pl.ds(k, 1)], sem.at[choice])

    for choice in range(TOP_K):
        def start(k, carry, choice=choice):
            row_copy(k, choice).start()
            return carry

        lax.fori_loop(0, tm, start, 0)
    for choice in range(TOP_K):
        def wait(k, carry, choice=choice):
            row_copy(k, choice).wait()
            return carry

        lax.fori_loop(0, tm, wait, 0)
    wt = wt_ref[...]
    o_ref[...] = x_ref[...] + wt[:, 0:1] * buf[0] + wt[:, 1:2] * buf[1]


def moe_combine(x, y, dest_t, wts, *, tm, tok0):
    m, k = x.shape
    n_tok = dest_t.shape[0] // TOP_K
    row = pl.BlockSpec((tm, k), lambda i, dest: (i, 0))
    return pl.pallas_call(
        functools.partial(_combine_kernel, tm=tm, n_tok=n_tok, tok0=tok0),
        out_shape=jax.ShapeDtypeStruct((m, k), F32),
        grid_spec=pltpu.PrefetchScalarGridSpec(
            num_scalar_prefetch=1,
            grid=(m // tm,),
            in_specs=[row, pl.BlockSpec((tm, LANES), lambda i, dest: (i, 0)), pl.BlockSpec(memory_space=pl.ANY)],
            out_specs=row,
            scratch_shapes=[pltpu.VMEM((TOP_K, tm, k), F32), pltpu.SemaphoreType.DMA((TOP_K,))]),
        compiler_params=_params(("arbitrary",)),
        name="moe_combine",
    )(dest_t, x, wts, y)


def _routing_tables(ids, *, tm):
    n_assign = ids.shape[0] * TOP_K
    n_tiles = (n_assign + N_EXPERTS * (tm - 1)) // tm
    flat = ids.reshape(-1)
    onehot = (flat[:, None] == jnp.arange(N_EXPERTS, dtype=jnp.int32)[None, :]).astype(jnp.int32)
    counts = jnp.sum(onehot, axis=0)
    rank = jnp.take_along_axis(jnp.cumsum(onehot, axis=0), flat[:, None], axis=1)[:, 0] - 1
    tiles_per = (counts + tm - 1) // tm
    tile_end = jnp.cumsum(tiles_per)
    row_start = (tile_end - tiles_per) * tm
    dest = row_start[flat] + rank
    n_used = tile_end[-1]
    tile_ids = jnp.arange(n_tiles, dtype=jnp.int32)
    tile_expert = jnp.minimum(jnp.searchsorted(tile_end, tile_ids, side="right"), N_EXPERTS - 1).astype(jnp.int32)
    last_used = tile_expert[jnp.maximum(n_used - 1, 0)]
    tile_expert = jnp.where(tile_ids < n_used, tile_expert, last_used)
    src = jnp.zeros((n_tiles * tm,), jnp.int32).at[dest].set(jnp.arange(n_assign, dtype=jnp.int32) // TOP_K)
    return src, dest.reshape(-1, TOP_K), tile_expert, n_used.reshape(1).astype(jnp.int32)


def hier_moe_layer(xp, xs, layer, norm_ffn, router_w, router_b, w_gate, w_up, w_down, *, tm_e=512, fc=256):
    hp, idp, wtp = moe_router(xp, norm_ffn[layer], router_w, router_b, tm=512)
    hs, ids_, wts_ = moe_router(xs, norm_ffn[layer], router_w, router_b, tm=xs.shape[0])
    h_all = jnp.concatenate([hp, hs], axis=0)
    ids = jnp.concatenate([idp[:, :TOP_K], ids_[:, :TOP_K]], axis=0)
    src, dest, tile_expert, n_used = _routing_tables(ids, tm=tm_e)
    x_sorted = jnp.take(h_all, src, axis=0)
    y = moe_experts(x_sorted, tile_expert, n_used, w_gate, w_up, w_down, layer=layer, tm=tm_e, fc=fc)
    dest_t = dest.T.reshape(-1)
    xp = moe_combine(xp, y, dest_t, wtp, tm=256, tok0=0)
    xs = moe_combine(xs, y, dest_t, wts_, tm=xs.shape[0], tok0=xp.shape[0])
    return xp, xs


def _pad_cols(w, n):
    return jnp.pad(w, ((0, 0), (0, n - w.shape[1])))


def kernel(x_prompt, x_sample, cache_a_k, cache_a_v, state_rwkv, state_shift, cache_sb_k, cache_sb_v, page_table,
           norm_mix, norm_ffn, norm_final, w_in0, lam_q1, lam_k1, lam_q2, lam_k2, subln0,
           rw_mu, rw_w0, rw_w2, rw_a0, rw_a2, rw_g2, rw_k_k, rw_k_a, rw_r_k, rw_ln_w, rw_ln_b, w_out0,
           w_qkv1, w_out1, router_grp_w, router_grp_b, router_exp_w, router_exp_b,
           exp_w_gate, exp_w_up, exp_w_down):
    batch, seq, d = x_prompt.shape
    s_n = x_sample.shape[0]
    n_p = batch * seq
    xp = x_prompt.reshape(n_p, d)
    xs = x_sample.reshape(s_n, d)
    n_phys = cache_a_k.shape[0]
    tm_p = 512

    lam_init = 0.8 - 0.6 * math.exp(-0.3 * 0)
    lam_rows = [z.reshape(1, A_HEAD_DIM) for z in (lam_q1, lam_k1, lam_q2, lam_k2)]
    w_q = w_in0[:, :A_WIDTH].astype(BF16)
    w_k = w_in0[:, A_WIDTH:2 * A_WIDTH].astype(BF16)
    w_v = w_in0[:, 2 * A_WIDTH:3 * A_WIDTH].astype(BF16)
    w_p = _pad_cols(w_in0[:, 3 * A_WIDTH:], B_PROJ_PAD).astype(BF16)
    g0 = norm_mix[0]

    (qp16,) = norm_matmul(xp, g0, w_q, (BF16,), tm=tm_p, tn=512)
    kp32, kp16 = norm_matmul(xp, g0, w_k, (F32, BF16), tm=tm_p, tn=512)
    vp32, vp16 = norm_matmul(xp, g0, w_v, (F32, BF16), tm=tm_p, tn=512)
    (pp,) = norm_matmul(xp, g0, w_p, (F32,), tm=tm_p, tn=1152)
    (qs32,) = norm_matmul(xs, g0, w_q, (F32,), tm=s_n, tn=512)
    (ks32,) = norm_matmul(xs, g0, w_k, (F32,), tm=s_n, tn=512)
    (vs32,) = norm_matmul(xs, g0, w_v, (F32,), tm=s_n, tn=512)
    (ps,) = norm_matmul(xs, g0, w_p, (F32,), tm=s_n, tn=1152)

    oa_p = diff_attn_prompt(qp16, kp16, vp16, lam_rows, subln0, batch=batch, seq=seq, lam_init=lam_init)
    oa_s = diff_attn_sample(qs32, ks32, vs32, cache_a_k.reshape(n_phys, PAGE_SIZE * A_HEADS, A_QK),
                            cache_a_v.reshape(n_phys, PAGE_SIZE * A_HEADS, A_QK), page_table, lam_rows, subln0,
                            lam_init=lam_init)

    zeros_tail = jnp.zeros((B_TAIL - B_DECAY_LORA - B_A_LORA - B_G_LORA, B_WIDTH), F32)
    w2_pad = jnp.concatenate([rw_w2, jnp.zeros((B_A_LORA + B_G_LORA, B_WIDTH), F32), zeros_tail], axis=0)
    a2_pad = jnp.concatenate([jnp.zeros((B_DECAY_LORA, B_WIDTH), F32), rw_a2,
                              jnp.zeros((B_G_LORA, B_WIDTH), F32), zeros_tail], axis=0)
    g2_pad = jnp.concatenate([jnp.zeros((B_DECAY_LORA + B_A_LORA, B_WIDTH), F32), rw_g2, zeros_tail], axis=0)
    row = lambda z: z.reshape(1, -1)
    prep_w = (row(jnp.pad(rw_mu, (0, B_PROJ_PAD - B_PROJ))), row(rw_w0), row(rw_a0), row(rw_k_k), row(rw_k_a),
              row(rw_r_k), w2_pad, a2_pad, g2_pad)

    r_n, d_n, k_n, na_n, b_n, v_n, g_n, bonus_n = rwkv_prep(
        pp, jnp.zeros((batch, 1, B_PROJ_PAD), F32), prep_w, batch=batch, seq=seq, tm=256, sequential=True)
    y_p, rwkv_p = rwkv_scan(na_n, r_n, d_n, b_n, k_n, v_n)
    ob_p = rwkv_post(y_p.reshape(n_p, B_WIDTH), bonus_n.reshape(n_p, B_WIDTH), g_n.reshape(n_p, B_WIDTH),
                     rw_ln_w, rw_ln_b, tm=tm_p)

    prev_s = jnp.pad(state_shift.reshape(s_n, B_PROJ), ((0, 0), (0, B_PROJ_PAD - B_PROJ)))
    r_s, d_s, k_s, na_s, b_s, v_s, g_s, bonus_s = (z.reshape(s_n, B_WIDTH) for z in rwkv_prep(
        ps, prev_s, prep_w, batch=s_n, seq=1, tm=s_n, sequential=False))
    y_s, rwkv_s = rwkv_step(na_s, r_s, d_s, b_s, k_s, v_s, state_rwkv)
    ob_s = rwkv_post(y_s, bonus_s, g_s, rw_ln_w, rw_ln_b, tm=s_n)

    w_o0 = w_out0.astype(BF16)
    xp = matmul_residual([oa_p, ob_p], w_o0, xp, tm=tm_p, tn=512)
    xs = matmul_residual([oa_s, ob_s], w_o0, xs, tm=s_n, tn=512)

    shift_p = pp.reshape(batch, seq, B_PROJ_PAD)[:, seq - 1:, :B_PROJ]
    shift_s = ps[:, :B_PROJ].reshape(s_n, 1, B_PROJ)
    a_k_p = kp32.reshape(batch, seq, A_HEADS, A_QK)
    a_v_p = vp32.reshape(batch, seq, A_HEADS, A_QK)
    a_k_s = ks32.reshape(s_n, 1, A_HEADS, A_QK)
    a_v_s = vs32.reshape(s_n, 1, A_HEADS, A_QK)

    def router_weights(layer):
        w = jnp.concatenate([router_grp_w[layer], router_exp_w[layer]], axis=1)
        b = jnp.concatenate([router_grp_b[layer], router_exp_b[layer]], axis=0)
        return _pad_cols(w, LANES), jnp.pad(b, (0, LANES - b.shape[0])).reshape(1, LANES)

    rw0, rb0 = router_weights(0)
    xp, xs = hier_moe_layer(xp, xs, 0, norm_ffn, rw0, rb0, exp_w_gate, exp_w_up, exp_w_down)

    g1 = norm_mix[1]
    w_q1 = w_qkv1[:, :d].astype(BF16)
    w_k1 = w_qkv1[:, d:2 * d].astype(BF16)
    w_v1 = w_qkv1[:, 2 * d:].astype(BF16)
    (qp16,) = norm_matmul(xp, g1, w_q1, (BF16,), tm=tm_p, tn=512)
    kp32, kp16 = norm_matmul(xp, g1, w_k1, (F32, BF16), tm=tm_p, tn=512)
    vp32, vp16 = norm_matmul(xp, g1, w_v1, (F32, BF16), tm=tm_p, tn=512)
    (qs32,) = norm_matmul(xs, g1, w_q1, (F32,), tm=s_n, tn=512)
    (ks32,) = norm_matmul(xs, g1, w_k1, (F32,), tm=s_n, tn=512)
    (vs32,) = norm_matmul(xs, g1, w_v1, (F32,), tm=s_n, tn=512)

    oc_p = sb_attn_prompt(qp16, kp16, vp16, batch=batch, seq=seq)
    oc_s = sb_attn_sample(qs32, cache_sb_k.reshape(n_phys, PAGE_SIZE * C_HEADS, C_HEAD_DIM),
                          cache_sb_v.reshape(n_phys, PAGE_SIZE * C_HEADS, C_HEAD_DIM), page_table)
    w_o1 = w_out1.astype(BF16)
    xp = matmul_residual([oc_p], w_o1, xp, tm=tm_p, tn=512)
    xs = matmul_residual([oc_s], w_o1, xs, tm=s_n, tn=512)

    sb_k_p = kp32.reshape(batch, seq, C_HEADS, C_HEAD_DIM)
    sb_v_p = vp32.reshape(batch, seq, C_HEADS, C_HEAD_DIM)
    sb_k_s = ks32.reshape(s_n, 1, C_HEADS, C_HEAD_DIM)
    sb_v_s = vs32.reshape(s_n, 1, C_HEADS, C_HEAD_DIM)

    rw1, rb1 = router_weights(1)
    xp, xs = hier_moe_layer(xp, xs, 1, norm_ffn, rw1, rb1, exp_w_gate, exp_w_up, exp_w_down)

    y_prompt = rmsnorm_rows(xp, norm_final, tm=tm_p).reshape(batch, seq, d)
    y_sample = rmsnorm_rows(xs, norm_final, tm=s_n).reshape(s_n, 1, d)
    return (y_prompt, y_sample, a_k_p, a_v_p, a_k_s, a_v_s, rwkv_p, rwkv_s, shift_p, shift_s,
            sb_k_p, sb_v_p, sb_k_s, sb_v_s)
```

```python
import functools
import math

import jax
import jax.numpy as jnp
from jax import lax
from jax.experimental import pallas as pl
from jax.experimental.pallas import tpu as pltpu

F32 = jnp.float32
BF16 = jnp.bfloat16
HIGHEST = lax.Precision.HIGHEST

D_MODEL = 2048
A_WIDTH = 1024
A_HEAD_DIM = 64
A_HEADS = 8
A_QK = 128
B_WIDTH = 1024
B_HEAD = 64
B_HEADS = 16
B_DECAY_LORA = 64
B_A_LORA = 64
B_G_LORA = 160
B_PROJ = 3 * B_WIDTH + B_DECAY_LORA + B_A_LORA + B_G_LORA
B_PROJ_PAD = 3456
B_TAIL = B_PROJ_PAD - 3 * B_WIDTH
C_HEADS = 16
C_HEAD_DIM = 128
N_GROUPS = 4
EXPERTS_PER_GROUP = 8
N_EXPERTS = 32
TOP_K = 2
EXPERT_HIDDEN = 1024
PAGE_SIZE = 128
RMS_EPS = 1e-6
GN_EPS = 64e-5
NEG_BIG = -1e30

LANES = 128
SUBLANES = 8
V7X_VMEM_BYTES = 64 * 1024 * 1024
VMEM_LIMIT = 48 * 1024 * 1024
PAGE_BUFFERS = 4


def _params(sem, vmem=VMEM_LIMIT):
    return pltpu.CompilerParams(dimension_semantics=sem, vmem_limit_bytes=vmem)


def _dot(a, b):
    return jnp.dot(a, b, preferred_element_type=F32)


def _dot_nt(a, b):
    return lax.dot_general(a, b, (((1,), (1,)), ((), ())), preferred_element_type=F32)


def _dot_f32(a, b):
    return jnp.dot(a, b, preferred_element_type=F32, precision=HIGHEST)


def _norm_matmul_kernel(x_ref, g_ref, w_ref, *refs, n_out):
    outs, h_scr = refs[:n_out], refs[n_out]

    @pl.when(pl.program_id(1) == 0)
    def _():
        x = x_ref[...]
        y = x * lax.rsqrt(jnp.mean(x * x, axis=-1, keepdims=True) + RMS_EPS)
        h_scr[...] = (y * g_ref[...]).astype(BF16)

    acc = _dot(h_scr[...], w_ref[...])
    for o in outs:
        o[...] = acc.astype(o.dtype)


def norm_matmul(x, gain, w, out_dtypes, *, tm, tn):
    m, k = x.shape
    n = w.shape[1]
    assert m % tm == 0 and n % tn == 0
    outs = tuple(jax.ShapeDtypeStruct((m, n), dt) for dt in out_dtypes)
    return pl.pallas_call(
        functools.partial(_norm_matmul_kernel, n_out=len(outs)),
        out_shape=outs,
        grid=(m // tm, n // tn),
        in_specs=[pl.BlockSpec((tm, k), lambda i, j: (i, 0)),
                  pl.BlockSpec((1, k), lambda i, j: (0, 0)),
                  pl.BlockSpec((k, tn), lambda i, j: (0, j))],
        out_specs=tuple(pl.BlockSpec((tm, tn), lambda i, j: (i, j)) for _ in outs),
        scratch_shapes=[pltpu.VMEM((tm, k), BF16)],
        compiler_params=_params(("parallel", "arbitrary")),
        name="norm_matmul",
    )(x, gain.reshape(1, k), w)


def _matmul_res_kernel(*refs, n_a):
    a_refs, w_refs, r_ref, o_ref = refs[:n_a], refs[n_a:2 * n_a], refs[2 * n_a], refs[2 * n_a + 1]
    acc = r_ref[...]
    for a, w in zip(a_refs, w_refs):
        acc = acc + _dot(a[...], w[...])
    o_ref[...] = acc


def matmul_residual(a_list, w, resid, *, tm, tn):
    m, n = resid.shape
    n_a = len(a_list)
    kk = a_list[0].shape[1]
    assert all(a.shape == (m, kk) for a in a_list) and w.shape == (n_a * kk, n)
    in_specs = [pl.BlockSpec((tm, kk), lambda i, j: (i, 0)) for _ in a_list]
    in_specs += [pl.BlockSpec((kk, tn), lambda i, j, c=c: (c, j)) for c in range(n_a)]
    in_specs += [pl.BlockSpec((tm, tn), lambda i, j: (i, j))]
    return pl.pallas_call(
        functools.partial(_matmul_res_kernel, n_a=n_a),
        out_shape=jax.ShapeDtypeStruct((m, n), F32),
        grid=(m // tm, n // tn),
        in_specs=in_specs,
        out_specs=pl.BlockSpec((tm, tn), lambda i, j: (i, j)),
        compiler_params=_params(("parallel", "parallel")),
        name="matmul_residual",
    )(*a_list, *([w] * n_a), resid)


def _rmsnorm_kernel(x_ref, g_ref, o_ref):
    x = x_ref[...]
    y = x * lax.rsqrt(jnp.mean(x * x, axis=-1, keepdims=True) + RMS_EPS)
    o_ref[...] = y * g_ref[...]


def rmsnorm_rows(x, gain, *, tm):
    m, k = x.shape
    return pl.pallas_call(
        _rmsnorm_kernel,
        out_shape=jax.ShapeDtypeStruct((m, k), F32),
        grid=(m // tm,),
        in_specs=[pl.BlockSpec((tm, k), lambda i: (i, 0)), pl.BlockSpec((1, k), lambda i: (0, 0))],
        out_specs=pl.BlockSpec((tm, k), lambda i: (i, 0)),
        compiler_params=_params(("parallel",)),
        name="final_rmsnorm",
    )(x, gain.reshape(1, k))


def _lam_value(lq1, lk1, lq2, lk2, lam_init):
    s1 = jnp.sum(lq1[...] * lk1[...], axis=-1, keepdims=True)
    s2 = jnp.sum(lq2[...] * lk2[...], axis=-1, keepdims=True)
    return jnp.exp(s1) - jnp.exp(s2) + lam_init


def _subln(o, sub_ref, lam_init):
    y = o * lax.rsqrt(jnp.mean(o * o, axis=-1, keepdims=True) + RMS_EPS)
    return (y * sub_ref[...]) * (1.0 - lam_init)


def _softmax_step(s, m, l, acc, v):
    m_new = jnp.maximum(m, jnp.max(s, axis=-1, keepdims=True))
    alpha = jnp.exp(m - m_new)
    p = jnp.exp(s - m_new)
    l = alpha * l + jnp.sum(p, axis=-1, keepdims=True)
    acc = alpha * acc + _dot(p.astype(BF16), v)
    return m_new, l, acc


def _diff_attn_prompt_kernel(q_ref, k_ref, v_ref, lq1, lk1, lq2, lk2, sub_ref, o_ref, *, tq, lam_init):
    h = pl.program_id(1)
    i = pl.program_id(2)
    scale = A_HEAD_DIM ** -0.5
    q = q_ref[...]
    lane = lax.broadcasted_iota(jnp.int32, (tq, A_QK), 1)
    q1 = jnp.where(lane < A_HEAD_DIM, q, jnp.zeros_like(q))
    q2 = jnp.where(lane >= A_HEAD_DIM, q, jnp.zeros_like(q))
    hh = (h + 1).astype(F32) * (8.0 / A_HEADS)
    slope = jnp.exp2(jnp.zeros((1, tq), F32) - hh)
    qpos = (i * tq).astype(F32) + lax.broadcasted_iota(jnp.int32, (tq, 1), 0).astype(F32)
    kiota = lax.broadcasted_iota(jnp.int32, (1, tq), 1).astype(F32)

    def tile(j, carry, masked):
        m1, l1, a1, m2, l2, a2 = carry
        start = pl.multiple_of(j * tq, tq)
        kj = k_ref[pl.ds(start, tq), :]
        vj = v_ref[pl.ds(start, tq), :]
        dist = qpos - ((j * tq).astype(F32) + kiota)
        bias = slope * dist
        s1 = _dot_nt(q1, kj) * scale - bias
        s2 = _dot_nt(q2, kj) * scale - bias
        if masked:
            ok = dist >= 0.0
            s1 = jnp.where(ok, s1, NEG_BIG)
            s2 = jnp.where(ok, s2, NEG_BIG)
        m1, l1, a1 = _softmax_step(s1, m1, l1, a1, vj)
        m2, l2, a2 = _softmax_step(s2, m2, l2, a2, vj)
        return m1, l1, a1, m2, l2, a2

    z1 = jnp.zeros((tq, 1), F32)
    za = jnp.zeros((tq, A_QK), F32)
    init = (z1 + NEG_BIG, z1, za, z1 + NEG_BIG, z1, za)
    carry = lax.fori_loop(0, i, lambda j, c: tile(j, c, False), init)
    m1, l1, a1, m2, l2, a2 = tile(i, carry, True)
    lam = _lam_value(lq1, lk1, lq2, lk2, lam_init)
    o = a1 / l1 - lam * (a2 / l2)
    o_ref[...] = _subln(o, sub_ref, lam_init).astype(o_ref.dtype)


def diff_attn_prompt(q, k, v, lam_rows, subln, *, batch, seq, lam_init, tq=256):
    nq = seq // tq
    small = [pl.BlockSpec((1, A_HEAD_DIM), lambda b, h, i: (0, 0)) for _ in range(4)]
    return pl.pallas_call(
        functools.partial(_diff_attn_prompt_kernel, tq=tq, lam_init=lam_init),
        out_shape=jax.ShapeDtypeStruct((batch * seq, A_WIDTH), BF16),
        grid=(batch, A_HEADS, nq),
        in_specs=[pl.BlockSpec((tq, A_QK), lambda b, h, i: (b * nq + i, h)),
                  pl.BlockSpec((seq, A_QK), lambda b, h, i: (b, h)),
                  pl.BlockSpec((seq, A_QK), lambda b, h, i: (b, h))] + small
                 + [pl.BlockSpec((1, A_QK), lambda b, h, i: (0, 0))],
        out_specs=pl.BlockSpec((tq, A_QK), lambda b, h, i: (b * nq + i, h)),
        compiler_params=_params(("parallel", "parallel", "arbitrary")),
        name="diff_attn_prompt",
    )(q, k, v, *lam_rows, subln.reshape(1, A_QK))


def _page_pipeline(pt_ref, k_hbm, v_hbm, kbuf, vbuf, sem, *, n_steps, pps, page_of):
    t = pl.program_id(0) * n_steps + pl.program_id(1)
    total = pl.num_programs(0) * n_steps
    depth = PAGE_BUFFERS - 1

    def copies(tt):
        seq, step, slot = tt // n_steps, tt % n_steps, tt % PAGE_BUFFERS
        out = []
        for n in range(pps):
            page = pt_ref[seq, page_of(step, n)]
            out.append(pltpu.make_async_copy(k_hbm.at[page], kbuf.at[slot, n], sem.at[slot, 0, n]))
            out.append(pltpu.make_async_copy(v_hbm.at[page], vbuf.at[slot, n], sem.at[slot, 1, n]))
        return out

    @pl.when(t == 0)
    def _():
        for d in range(depth):
            for c in copies(d):
                c.start()

    @pl.when(t + depth < total)
    def _():
        for c in copies(t + depth):
            c.start()

    for c in copies(t):
        c.wait()
    return t % PAGE_BUFFERS


def _page_pipeline_scratch(pps, n_keys, width):
    return [pltpu.VMEM((PAGE_BUFFERS, pps, n_keys, width), F32), pltpu.VMEM((PAGE_BUFFERS, pps, n_keys, width), F32),
            pltpu.SemaphoreType.DMA((PAGE_BUFFERS, 2, pps))]


def _diff_attn_sample_kernel(pt_ref, q_ref, kn_ref, vn_ref, k_hbm, v_hbm, lq1, lk1, lq2, lk2, sub_ref,
                             o_ref, m_scr, l_scr, acc_scr, kbuf, vbuf, sem, *, n_pages, pps, lam_init):
    slot = _page_pipeline(pt_ref, k_hbm, v_hbm, kbuf, vbuf, sem, n_steps=n_pages // pps, pps=pps,
                          page_of=lambda step, n: step * pps + n)
    p = pl.program_id(1)
    rows = 2 * A_HEADS
    n_keys = PAGE_SIZE * A_HEADS * pps
    scale = A_HEAD_DIM ** -0.5

    @pl.when(p == 0)
    def _():
        m_scr[...] = jnp.full((rows, 1), NEG_BIG, F32)
        l_scr[...] = jnp.zeros((rows, 1), F32)
        acc_scr[...] = jnp.zeros((rows, A_QK), F32)

    q = q_ref[0]
    lane = lax.broadcasted_iota(jnp.int32, (A_HEADS, A_QK), 1)
    qm = jnp.concatenate([jnp.where(lane < A_HEAD_DIM, q, 0.0), jnp.where(lane >= A_HEAD_DIM, q, 0.0)], axis=0)
    row = lax.broadcasted_iota(jnp.int32, (rows, n_keys), 0)
    col = lax.broadcasted_iota(jnp.int32, (rows, n_keys), 1)
    own = (col % A_HEADS) == (row % A_HEADS)
    head1 = (lax.broadcasted_iota(jnp.int32, (rows, 1), 0) % A_HEADS + 1).astype(F32) * (8.0 / A_HEADS)
    slope = jnp.exp2(-head1)
    kpos = (p * (PAGE_SIZE * pps)).astype(F32) + (col // A_HEADS).astype(F32)
    dist = float(n_pages * PAGE_SIZE) - kpos
    qb = qm.astype(BF16)
    s = jnp.concatenate([_dot_nt(qb, kbuf[slot, n].astype(BF16)) for n in range(pps)], axis=1)
    s = jnp.where(own, s * scale - slope * dist, NEG_BIG)
    m0, l0 = m_scr[...], l_scr[...]
    m = jnp.maximum(m0, jnp.max(s, axis=-1, keepdims=True))
    alpha = jnp.exp(m0 - m)
    pr = jnp.exp(s - m)
    l = alpha * l0 + jnp.sum(pr, axis=-1, keepdims=True)
    acc = alpha * acc_scr[...]
    page_keys = PAGE_SIZE * A_HEADS
    for n in range(pps):
        acc = acc + _dot(pr[:, n * page_keys:(n + 1) * page_keys].astype(BF16), vbuf[slot, n].astype(BF16))
    m_scr[...] = m
    l_scr[...] = l
    acc_scr[...] = acc

    @pl.when(p == n_pages // pps - 1)
    def _():
        kn2 = jnp.concatenate([kn_ref[0], kn_ref[0]], axis=0)
        vn2 = jnp.concatenate([vn_ref[0], vn_ref[0]], axis=0)
        s_new = jnp.sum(qm * kn2, axis=-1, keepdims=True) * scale
        m_new = jnp.maximum(m, s_new)
        alpha = jnp.exp(m - m_new)
        p_new = jnp.exp(s_new - m_new)
        o16 = (alpha * acc + p_new * vn2) / (alpha * l + p_new)
        lam = _lam_value(lq1, lk1, lq2, lk2, lam_init)
        o = o16[:A_HEADS] - lam * o16[A_HEADS:]
        o_ref[0] = _subln(o, sub_ref, lam_init).astype(o_ref.dtype)


def diff_attn_sample(q, k_new, v_new, cache_k, cache_v, page_table, lam_rows, subln, *, lam_init):
    s_n, n_pages = page_table.shape
    pps = 2
    assert n_pages % pps == 0
    n_keys = PAGE_SIZE * A_HEADS
    small = [pl.BlockSpec((1, A_HEAD_DIM), lambda s, p, pt: (0, 0)) for _ in range(4)]
    row_spec = pl.BlockSpec((1, A_HEADS, A_QK), lambda s, p, pt: (s, 0, 0))
    hbm = pl.BlockSpec(memory_space=pl.ANY)
    rows = 2 * A_HEADS
    heads = lambda z: z.reshape(s_n, A_HEADS, A_QK)
    out = pl.pallas_call(
        functools.partial(_diff_attn_sample_kernel, n_pages=n_pages, pps=pps, lam_init=lam_init),
        out_shape=jax.ShapeDtypeStruct((s_n, A_HEADS, A_QK), BF16),
        grid_spec=pltpu.PrefetchScalarGridSpec(
            num_scalar_prefetch=1,
            grid=(s_n, n_pages // pps),
            in_specs=[row_spec, row_spec, row_spec, hbm, hbm] + small
                     + [pl.BlockSpec((1, A_QK), lambda s, p, pt: (0, 0))],
            out_specs=pl.BlockSpec((1, A_HEADS, A_QK), lambda s, p, pt: (s, 0, 0)),
            scratch_shapes=[pltpu.VMEM((rows, 1), F32), pltpu.VMEM((rows, 1), F32),
                            pltpu.VMEM((rows, A_QK), F32)] + _page_pipeline_scratch(pps, n_keys, A_QK)),
        compiler_params=_params(("arbitrary", "arbitrary")),
        name="diff_attn_sample",
    )(page_table, heads(q), heads(k_new), heads(v_new), cache_k, cache_v, *lam_rows, subln.reshape(1, A_QK))
    return out.reshape(s_n, A_WIDTH)


def _log_sigmoid(z):
    return jnp.minimum(z, 0.0) - jnp.log1p(jnp.exp(-jnp.abs(z)))


def _suffix_sum(lk, tri):
    hi = lk.astype(BF16)
    lo = (lk - hi.astype(F32)).astype(BF16)
    return _dot(hi, tri) + _dot(lo, tri)


def _strict_lower_ones(n):
    r = lax.broadcasted_iota(jnp.int32, (n, n), 0)
    c = lax.broadcasted_iota(jnp.int32, (n, n), 1)
    return jnp.where(r > c, 1.0, 0.0).astype(BF16)


def _sb_prompt_kernel(q_ref, k_ref, v_ref, o_ref, *, tq, hpg):
    i = pl.program_id(2)
    scale = C_HEAD_DIM ** -0.5
    tri = _strict_lower_ones(tq)
    rr = lax.broadcasted_iota(jnp.int32, (tq, tq), 0)
    cc = lax.broadcasted_iota(jnp.int32, (tq, tq), 1)
    before = cc < rr
    qs = [q_ref[:, h * C_HEAD_DIM:(h + 1) * C_HEAD_DIM] for h in range(hpg)]

    def tile(j, carry, masked):
        start = pl.multiple_of(j * tq, tq)
        out = []
        for h in range(hpg):
            c, acc = carry[h]
            lanes = slice(h * C_HEAD_DIM, (h + 1) * C_HEAD_DIM)
            kj = k_ref[pl.ds(start, tq), lanes]
            vj = v_ref[pl.ds(start, tq), lanes]
            z = _dot_nt(qs[h], kj) * scale
            ls = _log_sigmoid(z)
            lk = ls - z
            if masked:
                lk = jnp.where(before, lk, 0.0)
            later = c + _suffix_sum(lk, tri)
            att = jnp.exp(ls + later)
            if masked:
                att = jnp.where(before, att, 0.0)
            acc = acc + _dot(att.astype(BF16), vj)
            c = c + jnp.sum(lk, axis=-1, keepdims=True)
            out.append((c, acc))
        return tuple(out)

    zero = (jnp.zeros((tq, 1), F32), jnp.zeros((tq, C_HEAD_DIM), F32))
    carry = tile(i, (zero,) * hpg, True)
    carry = lax.fori_loop(0, i, lambda jj, c: tile(i - 1 - jj, c, False), carry)
    o_ref[...] = jnp.concatenate([acc for _, acc in carry], axis=1).astype(o_ref.dtype)


def sb_attn_prompt(q, k, v, *, batch, seq, tq=256, hpg=2):
    nq = seq // tq
    width = C_HEADS * C_HEAD_DIM
    wl = hpg * C_HEAD_DIM
    return pl.pallas_call(
        functools.partial(_sb_prompt_kernel, tq=tq, hpg=hpg),
        out_shape=jax.ShapeDtypeStruct((batch * seq, width), BF16),
        grid=(batch, C_HEADS // hpg, nq),
        in_specs=[pl.BlockSpec((tq, wl), lambda b, h, i: (b * nq + i, h)),
                  pl.BlockSpec((seq, wl), lambda b, h, i: (b, h)),
                  pl.BlockSpec((seq, wl), lambda b, h, i: (b, h))],
        out_specs=pl.BlockSpec((tq, wl), lambda b, h, i: (b * nq + i, h)),
        compiler_params=_params(("parallel", "parallel", "arbitrary")),
        name="sb_attn_prompt",
    )(q, k, v)


def _sb_sample_kernel(pt_ref, q_ref, k_hbm, v_hbm, o_ref, c_scr, acc_scr, kbuf, vbuf, sem, *, n_pages):
    p = pl.program_id(1)
    n_tiles = PAGE_SIZE * C_HEADS // LANES
    scale = C_HEAD_DIM ** -0.5

    @pl.when(p == 0)
    def _():
        c_scr[...] = jnp.zeros((1, LANES), F32)
        acc_scr[...] = jnp.zeros((C_HEADS, C_HEAD_DIM), F32)

    slot = _page_pipeline(pt_ref, k_hbm, v_hbm, kbuf, vbuf, sem, n_steps=n_pages, pps=1,
                          page_of=lambda step, n: n_pages - 1 - step)
    kf = kbuf[slot, 0].astype(BF16)
    vf = vbuf[slot, 0].astype(BF16)
    zt = _dot_nt(q_ref[0].astype(BF16), kf)
    row = lax.broadcasted_iota(jnp.int32, (C_HEADS, LANES), 0)
    lane = lax.broadcasted_iota(jnp.int32, (C_HEADS, LANES), 1)
    own = (lane % C_HEADS) == row
    z = jnp.concatenate([jnp.sum(jnp.where(own, zt[:, g * LANES:(g + 1) * LANES], 0.0), axis=0, keepdims=True)
                         for g in range(n_tiles)], axis=0) * scale
    ls = _log_sigmoid(z)
    lk = ls - z
    li = lax.broadcasted_iota(jnp.int32, (LANES, LANES), 0)
    lj = lax.broadcasted_iota(jnp.int32, (LANES, LANES), 1)
    same = (li % C_HEADS) == (lj % C_HEADS)
    hi = lk.astype(BF16)
    lo = (lk - hi.astype(F32)).astype(BF16)
    sel = jnp.concatenate([jnp.where(same & (li > lj), 1.0, 0.0), jnp.where(same, 1.0, 0.0)], axis=1).astype(BF16)
    both = _dot(hi, sel) + _dot(lo, sel)
    inside, total = both[:, :LANES], both[:, LANES:]
    run = c_scr[...]
    later = [None] * n_tiles
    for g in range(n_tiles - 1, -1, -1):
        later[g] = run + inside[g:g + 1]
        run = run + total[g:g + 1]
    c_scr[...] = run
    att = jnp.exp(ls + jnp.concatenate(later, axis=0))
    spread = jnp.concatenate([jnp.where(own, att[g:g + 1], 0.0) for g in range(n_tiles)], axis=1)
    acc = acc_scr[...] + _dot(spread.astype(BF16), vf)
    acc_scr[...] = acc

    @pl.when(p == pl.num_programs(1) - 1)
    def _():
        o_ref[0] = acc.astype(o_ref.dtype)


def sb_attn_sample(q, cache_k, cache_v, page_table):
    s_n, n_pages = page_table.shape
    n_keys = PAGE_SIZE * C_HEADS
    hbm = pl.BlockSpec(memory_space=pl.ANY)
    out = pl.pallas_call(
        functools.partial(_sb_sample_kernel, n_pages=n_pages),
        out_shape=jax.ShapeDtypeStruct((s_n, C_HEADS, C_HEAD_DIM), BF16),
        grid_spec=pltpu.PrefetchScalarGridSpec(
            num_scalar_prefetch=1,
            grid=(s_n, n_pages),
            in_specs=[pl.BlockSpec((1, C_HEADS, C_HEAD_DIM), lambda s, p, pt: (s, 0, 0)), hbm, hbm],
            out_specs=pl.BlockSpec((1, C_HEADS, C_HEAD_DIM), lambda s, p, pt: (s, 0, 0)),
            scratch_shapes=[pltpu.VMEM((1, LANES), F32), pltpu.VMEM((C_HEADS, C_HEAD_DIM), F32)]
                           + _page_pipeline_scratch(1, n_keys, C_HEAD_DIM)),
        compiler_params=_params(("arbitrary", "arbitrary")),
        name="sb_attn_sample",
    )(page_table, q.reshape(s_n, C_HEADS, C_HEAD_DIM), cache_k, cache_v)
    return out.reshape(s_n, C_HEADS * C_HEAD_DIM)


def _head_sum(x):
    r = lax.broadcasted_iota(jnp.int32, (LANES, LANES), 0) // B_HEAD
    c = lax.broadcasted_iota(jnp.int32, (LANES, LANES), 1) // B_HEAD
    ones = jnp.where(r == c, 1.0, 0.0).astype(F32)
    parts = [_dot_f32(x[:, g * LANES:(g + 1) * LANES], ones) for g in range(x.shape[1] // LANES)]
    return jnp.concatenate(parts, axis=1)


def _softplus(x):
    return jnp.maximum(x, 0.0) + jnp.log1p(jnp.exp(-jnp.abs(x)))


def _sigmoid(x):
    return 1.0 / (1.0 + jnp.exp(-x))


def _rwkv_prep_kernel(p_ref, prev_ref, mu_ref, w0_ref, a0_ref, kk_ref, ka_ref, rk_ref, w2_ref, a2_ref, g2_ref,
                      *refs, sequential):
    outs = refs[:8]
    p = p_ref[...]
    tm = p.shape[0]
    if sequential:
        last_scr = refs[8]
        t = pl.program_id(1)

        @pl.when(t == 0)
        def _():
            last_scr[...] = prev_ref[0]

        rolled = pltpu.roll(p, 1, axis=0)
        row = lax.broadcasted_iota(jnp.int32, (tm, 1), 0)
        shifted = jnp.where(row == 0, last_scr[...], rolled)
        last_scr[...] = p[tm - 1:tm, :]
    else:
        shifted = prev_ref[...]
    xs = p + (shifted - p) * mu_ref[...]
    r = xs[:, :B_WIDTH]
    k = xs[:, B_WIDTH:2 * B_WIDTH]
    v = xs[:, 2 * B_WIDTH:3 * B_WIDTH]
    tail = xs[:, 3 * B_WIDTH:]
    w = -_softplus(-(w0_ref[...] + _dot_f32(jnp.tanh(tail), w2_ref[...]))) - 0.5
    decay = jnp.exp(-jnp.exp(w))
    a = _sigmoid(a0_ref[...] + _dot_f32(tail, a2_ref[...]))
    g = _dot_f32(_sigmoid(tail), g2_ref[...])
    kk = k * kk_ref[...]
    kk = kk * lax.rsqrt(jnp.maximum(_head_sum(kk * kk), 1e-24))
    k_h = k * (1.0 + (a - 1.0) * ka_ref[...])
    bonus = _head_sum(r * k_h * rk_ref[...]) * v
    r_o, d_o, k_o, na_o, b_o, v_o, g_o, bonus_o = outs
    r_o[0] = r
    d_o[0] = decay
    k_o[0] = k_h
    na_o[0] = -kk
    b_o[0] = kk * a
    v_o[0] = v
    g_o[0] = g
    bonus_o[0] = bonus


def rwkv_prep(p, prev, weights, *, batch, seq, tm, sequential):
    nt = seq // tm if sequential else 1
    nb = batch if sequential else (batch * seq) // tm
    vec = lambda n: pl.BlockSpec((1, n), lambda b, t: (0, 0))
    mat = lambda: pl.BlockSpec((B_TAIL, B_WIDTH), lambda b, t: (0, 0))
    if sequential:
        p_spec = pl.BlockSpec((tm, B_PROJ_PAD), lambda b, t: (b * nt + t, 0))
        prev_spec = pl.BlockSpec((1, 1, B_PROJ_PAD), lambda b, t: (b, 0, 0))
        n_shape = jax.ShapeDtypeStruct((batch, seq, B_WIDTH), F32)
        n_spec = pl.BlockSpec((1, tm, B_WIDTH), lambda b, t: (b, t, 0))
        scratch = [pltpu.VMEM((1, B_PROJ_PAD), F32)]
    else:
        p_spec = pl.BlockSpec((tm, B_PROJ_PAD), lambda b, t: (b, 0))
        prev_spec = pl.BlockSpec((tm, B_PROJ_PAD), lambda b, t: (b, 0))
        n_shape = jax.ShapeDtypeStruct((nb, tm, B_WIDTH), F32)
        n_spec = pl.BlockSpec((1, tm, B_WIDTH), lambda b, t: (b, 0, 0))
        scratch = []
    return pl.pallas_call(
        functools.partial(_rwkv_prep_kernel, sequential=sequential),
        out_shape=(n_shape,) * 8,
        grid=(nb, nt),
        in_specs=[p_spec, prev_spec, vec(B_PROJ_PAD), vec(B_WIDTH), vec(B_WIDTH), vec(B_WIDTH), vec(B_WIDTH),
                  vec(B_WIDTH), mat(), mat(), mat()],
        out_specs=(n_spec,) * 8,
        scratch_shapes=scratch,
        compiler_params=_params(("parallel", "arbitrary")),
        name="rwkv_prep",
    )(p, prev, *weights)


def _rwkv_scan_kernel(a_ref, r_ref, d_ref, b_ref, k_ref, v_ref, y_ref, s_ref, st_scr, *, nb, tc, n_groups):
    t_idx = pl.program_id(1)
    n_pairs = B_HEADS // 2

    @pl.when(t_idx == 0)
    def _():
        st_scr[...] = jnp.zeros(st_scr.shape, F32)

    r_i = lax.broadcasted_iota(jnp.int32, (B_HEAD, LANES), 0)
    l_i = lax.broadcasted_iota(jnp.int32, (B_HEAD, LANES), 1)
    diag = (l_i % B_HEAD) == r_i
    kr = lax.broadcasted_iota(jnp.int32, (LANES, LANES), 0) // B_HEAD
    kc = lax.broadcasted_iota(jnp.int32, (LANES, LANES), 1) // B_HEAD
    ones_blk = jnp.where(kr == kc, 1.0, 0.0).astype(BF16)
    chains = [(bb, pr) for bb in range(nb) for pr in range(n_pairs)]
    per = len(chains) // n_groups
    groups = [chains[g * per:(g + 1) * per] for g in range(n_groups)]
    refs = dict(a=a_ref, r=r_ref, d=d_ref, b=b_ref, k=k_ref, v=v_ref)

    def block8(blk, carry):
        t0 = pl.multiple_of(blk * SUBLANES, SUBLANES)
        rows8 = {n: [ref[bb, pl.ds(t0, SUBLANES), :] for bb in range(nb)] for n, ref in refs.items()}
        y_rows = [[[None] * n_pairs for _ in range(SUBLANES)] for _ in range(nb)]
        for cc in range(SUBLANES):
            for grp in groups:
                row = lambda n, bb, pr: rows8[n][bb][cc:cc + 1, pr * LANES:(pr + 1) * LANES]
                lhs = []
                for bb, pr in grp:
                    st = st_scr[bb * n_pairs + pr]
                    lhs.append((st * row("a", bb, pr)).astype(BF16))
                    lhs.append(jnp.where(diag, row("v", bb, pr), 0.0).astype(BF16))
                res = _dot(jnp.concatenate(lhs, axis=0), ones_blk)
                lhs = []
                for n, (bb, pr) in enumerate(grp):
                    sa_b = res[n * LANES:n * LANES + B_HEAD]
                    v_b = res[n * LANES + B_HEAD:(n + 1) * LANES]
                    st = st_scr[bb * n_pairs + pr]
                    st = st * row("d", bb, pr) + sa_b * row("b", bb, pr) + v_b * row("k", bb, pr)
                    st_scr[bb * n_pairs + pr] = st
                    lhs.append((st * row("r", bb, pr)).astype(BF16))
                res = _dot(jnp.concatenate(lhs, axis=0), ones_blk)
                for n, (bb, pr) in enumerate(grp):
                    y_b = res[n * B_HEAD:(n + 1) * B_HEAD]
                    y_rows[bb][cc][pr] = jnp.sum(jnp.where(diag, y_b, 0.0), axis=0, keepdims=True)
        for bb in range(nb):
            y8 = [jnp.concatenate(y_rows[bb][cc], axis=1) for cc in range(SUBLANES)]
            y_ref[bb, pl.ds(t0, SUBLANES), :] = jnp.concatenate(y8, axis=0)
        return carry

    lax.fori_loop(0, tc // SUBLANES, block8, 0)

    @pl.when(t_idx == pl.num_programs(1) - 1)
    def _():
        for bb, pr in chains:
            st = st_scr[bb * n_pairs + pr]
            for half in range(2):
                s_ref[bb, 2 * pr + half] = st[:, half * B_HEAD:(half + 1) * B_HEAD]


def rwkv_scan(na, r, d, b, k, v, *, nb=2, tc=128, n_groups=1):
    batch, seq, _ = r.shape
    spec = pl.BlockSpec((nb, tc, B_WIDTH), lambda g, t: (g, t, 0))
    return pl.pallas_call(
        functools.partial(_rwkv_scan_kernel, nb=nb, tc=tc, n_groups=n_groups),
        out_shape=(jax.ShapeDtypeStruct((batch, seq, B_WIDTH), F32),
                   jax.ShapeDtypeStruct((batch, B_HEADS, B_HEAD, B_HEAD), F32)),
        grid=(batch // nb, seq // tc),
        in_specs=[spec] * 6,
        out_specs=(spec, pl.BlockSpec((nb, B_HEADS, B_HEAD, B_HEAD), lambda g, t: (g, 0, 0, 0))),
        scratch_shapes=[pltpu.VMEM((nb * B_HEADS // 2, B_HEAD, LANES), F32)],
        compiler_params=_params(("parallel", "arbitrary")),
        name="rwkv_scan",
    )(na, r, d, b, k, v)


def _rwkv_step_kernel(a_ref, r_ref, d_ref, b_ref, k_ref, v_ref, s0_ref, y_ref, s_ref):
    r_i = lax.broadcasted_iota(jnp.int32, (B_HEAD, B_HEAD), 0)
    c_i = lax.broadcasted_iota(jnp.int32, (B_HEAD, B_HEAD), 1)
    eye = jnp.where(r_i == c_i, 1.0, 0.0).astype(F32)
    for h in range(B_HEADS):
        row = lambda ref: ref[0, h:h + 1, :]
        st = s0_ref[0, h]
        sa = jnp.sum(st * row(a_ref), axis=1, keepdims=True)
        v_col = jnp.sum(eye * row(v_ref), axis=1, keepdims=True)
        st = st * row(d_ref) + sa * row(b_ref) + v_col * row(k_ref)
        s_ref[0, h] = st
        y_col = jnp.sum(st * row(r_ref), axis=1, keepdims=True)
        y_ref[0, h:h + 1, :] = jnp.sum(eye * y_col, axis=0, keepdims=True)


def rwkv_step(na, r, d, b, k, v, s0):
    s_n = s0.shape[0]
    hs = lambda z: z.reshape(s_n, B_HEADS, B_HEAD)
    vec = pl.BlockSpec((1, B_HEADS, B_HEAD), lambda s: (s, 0, 0))
    st = pl.BlockSpec((1, B_HEADS, B_HEAD, B_HEAD), lambda s: (s, 0, 0, 0))
    y, s_new = pl.pallas_call(
        _rwkv_step_kernel,
        out_shape=(jax.ShapeDtypeStruct((s_n, B_HEADS, B_HEAD), F32),
                   jax.ShapeDtypeStruct(s0.shape, F32)),
        grid=(s_n,),
        in_specs=[vec] * 6 + [st],
        out_specs=(vec, st),
        compiler_params=_params(("parallel",)),
        name="rwkv_step",
    )(hs(na), hs(r), hs(d), hs(b), hs(k), hs(v), s0)
    return y.reshape(s_n, B_WIDTH), s_new


def _rwkv_post_kernel(y_ref, bonus_ref, g_ref, lw_ref, lb_ref, o_ref):
    y = y_ref[...]
    mean = _head_sum(y) * (1.0 / B_HEAD)
    yc = y - mean
    var = _head_sum(yc * yc) * (1.0 / B_HEAD)
    out = yc * lax.rsqrt(var + GN_EPS) * lw_ref[...] + lb_ref[...]
    o_ref[...] = ((out + bonus_ref[...]) * g_ref[...]).astype(o_ref.dtype)


def rwkv_post(y, bonus, g, ln_w, ln_b, *, tm):
    m = y.shape[0]
    row = pl.BlockSpec((tm, B_WIDTH), lambda i: (i, 0))
    vec = pl.BlockSpec((1, B_WIDTH), lambda i: (0, 0))
    return pl.pallas_call(
        _rwkv_post_kernel,
        out_shape=jax.ShapeDtypeStruct((m, B_WIDTH), BF16),
        grid=(m // tm,),
        in_specs=[row, row, row, vec, vec],
        out_specs=row,
        compiler_params=_params(("parallel",)),
        name="rwkv_post",
    )(y, bonus, g, ln_w.reshape(1, B_WIDTH), ln_b.reshape(1, B_WIDTH))


def _router_kernel(x_ref, g_ref, w_ref, b_ref, id_ref, wt_ref):
    x = x_ref[...]
    tm = x.shape[0]
    h = (x * lax.rsqrt(jnp.mean(x * x, axis=-1, keepdims=True) + RMS_EPS)) * g_ref[...]
    logits = _dot_f32(h, w_ref[...]) + b_ref[...]
    lane = lax.broadcasted_iota(jnp.int32, (tm, LANES), 1).astype(F32)
    none = float(LANES)
    is_grp = lane < N_GROUPS
    g_max = jnp.max(jnp.where(is_grp, logits, NEG_BIG), axis=-1, keepdims=True)
    grp = jnp.min(jnp.where(is_grp & (logits == g_max), lane, none), axis=-1, keepdims=True)
    p_grp = 1.0 / jnp.sum(jnp.where(is_grp, jnp.exp(logits - g_max), 0.0), axis=-1, keepdims=True)
    lo = N_GROUPS + EXPERTS_PER_GROUP * grp
    in_grp = (lane >= lo) & (lane < lo + EXPERTS_PER_GROUP)
    v1 = jnp.max(jnp.where(in_grp, logits, NEG_BIG), axis=-1, keepdims=True)
    i1 = jnp.min(jnp.where(in_grp & (logits == v1), lane, none), axis=-1, keepdims=True)
    rest = in_grp & (lane != i1)
    v2 = jnp.max(jnp.where(rest, logits, NEG_BIG), axis=-1, keepdims=True)
    i2 = jnp.min(jnp.where(rest & (logits == v2), lane, none), axis=-1, keepdims=True)
    e = jnp.exp(v2 - v1)
    w1 = p_grp / (1.0 + e)
    w2 = p_grp * e / (1.0 + e)
    ids = jnp.where(lane == 0.0, i1 - N_GROUPS, jnp.where(lane == 1.0, i2 - N_GROUPS, 0.0))
    id_ref[...] = ids.astype(jnp.int32)
    wt_ref[...] = jnp.where(lane == 0.0, w1, jnp.where(lane == 1.0, w2, 0.0))


def moe_router(x, gain, w_router, b_router, *, tm):
    m, k = x.shape
    return pl.pallas_call(
        _router_kernel,
        out_shape=(jax.ShapeDtypeStruct((m, LANES), jnp.int32), jax.ShapeDtypeStruct((m, LANES), F32)),
        grid=(m // tm,),
        in_specs=[pl.BlockSpec((tm, k), lambda i: (i, 0)), pl.BlockSpec((1, k), lambda i: (0, 0)),
                  pl.BlockSpec((k, LANES), lambda i: (0, 0)), pl.BlockSpec((1, LANES), lambda i: (0, 0))],
        out_specs=(pl.BlockSpec((tm, LANES), lambda i: (i, 0)), pl.BlockSpec((tm, LANES), lambda i: (i, 0))),
        compiler_params=_params(("parallel",)),
        name="moe_router",
    )(x, gain.reshape(1, k), w_router, b_router)


def _dispatch_kernel(src_ref, nrow_ref, g_ref, x_hbm, o_ref, buf, sem, *, tm):
    i = pl.program_id(0)
    live = i * tm < nrow_ref[0]

    def row_copy(k):
        return pltpu.make_async_copy(x_hbm.at[pl.ds(src_ref[i * tm + k], 1)], buf.at[pl.ds(k, 1)], sem.at[0])

    @pl.when(live)
    def _():
        def start(k, carry):
            row_copy(k).start()
            return carry

        def wait(k, carry):
            row_copy(k).wait()
            return carry

        lax.fori_loop(0, tm, start, 0)
        lax.fori_loop(0, tm, wait, 0)
        x = buf[...]
        h = (x * lax.rsqrt(jnp.mean(x * x, axis=-1, keepdims=True) + RMS_EPS)) * g_ref[...]
        o_ref[...] = h.astype(o_ref.dtype)

    @pl.when(jnp.logical_not(live))
    def _():
        o_ref[...] = jnp.zeros(o_ref.shape, o_ref.dtype)


def moe_dispatch(x_all, gain, src, n_rows_used, *, tm):
    rows = src.shape[0]
    k = x_all.shape[1]
    return pl.pallas_call(
        functools.partial(_dispatch_kernel, tm=tm),
        out_shape=jax.ShapeDtypeStruct((rows, k), BF16),
        grid_spec=pltpu.PrefetchScalarGridSpec(
            num_scalar_prefetch=2,
            grid=(rows // tm,),
            in_specs=[pl.BlockSpec((1, k), lambda i, s, n: (0, 0)), pl.BlockSpec(memory_space=pl.ANY)],
            out_specs=pl.BlockSpec((tm, k), lambda i, s, n: (i, 0)),
            scratch_shapes=[pltpu.VMEM((tm, k), F32), pltpu.SemaphoreType.DMA((1,))]),
        compiler_params=_params(("arbitrary",)),
        name="moe_dispatch",
    )(src, n_rows_used, gain.reshape(1, k), x_all)


def _expert_kernel(te_ref, nu_ref, x_ref, wg_ref, wu_ref, wd_ref, o_ref):
    i = pl.program_id(0)
    f = pl.program_id(1)
    used = i < nu_ref[0]

    @pl.when(used)
    def _():
        x = x_ref[...]
        hg = _dot(x, wg_ref[0, 0].astype(BF16))
        hu = _dot(x, wu_ref[0, 0].astype(BF16))
        act = (hg * _sigmoid(hg)) * hu
        part = _dot(act.astype(BF16), wd_ref[0, 0].astype(BF16))

        @pl.when(f == 0)
        def _():
            o_ref[...] = part

        @pl.when(f > 0)
        def _():
            o_ref[...] += part

    @pl.when(jnp.logical_not(used) & (f == 0))
    def _():
        o_ref[...] = jnp.zeros(o_ref.shape, F32)


def moe_experts(xs, tile_expert, n_used, w_gate, w_up, w_down, *, layer, tm, fc):
    rows, d = xs.shape
    n_tiles = rows // tm
    nf = EXPERT_HIDDEN // fc

    def live(i, f, te, nu):
        ok = i < nu[0]
        return jnp.where(ok, i, nu[0] - 1), jnp.where(ok, f, nf - 1)

    def x_map(i, f, te, nu):
        return live(i, f, te, nu)[0], 0

    def gu_map(i, f, te, nu):
        ii, ff = live(i, f, te, nu)
        return layer, te[ii], 0, ff

    def d_map(i, f, te, nu):
        ii, ff = live(i, f, te, nu)
        return layer, te[ii], ff, 0

    return pl.pallas_call(
        _expert_kernel,
        out_shape=jax.ShapeDtypeStruct((rows, d), F32),
        grid_spec=pltpu.PrefetchScalarGridSpec(
            num_scalar_prefetch=2,
            grid=(n_tiles, nf),
            in_specs=[pl.BlockSpec((tm, d), x_map),
                      pl.BlockSpec((1, 1, d, fc), gu_map),
                      pl.BlockSpec((1, 1, d, fc), gu_map),
                      pl.BlockSpec((1, 1, fc, d), d_map)],
            out_specs=pl.BlockSpec((tm, d), lambda i, f, te, nu: (i, 0))),
        compiler_params=_params(("arbitrary", "arbitrary"), vmem=56 * 1024 * 1024),
        name="moe_experts",
    )(tile_expert, n_used, xs, w_gate, w_up, w_down)


def _combine_kernel(dest_ref, x_ref, wt_ref, y_hbm, o_ref, buf, sem, *, tm, n_tok, tok0):
    i = pl.program_id(0)

    def row_copy(k, choice):
        src_row = dest_ref[choice * n_tok + tok0 + i * tm + k]
        return pltpu.make_async_copy(y_hbm.at[pl.ds(src_row, 1)], buf.at[choice, pl.ds(k, 1)], sem.at[choice])

    for choice in range(TOP_K):
        def start(k, carry, choice=choice):
            row_copy(k, choice).start()
            return carry

        lax.fori_loop(0, tm, start, 0)
    for choice in range(TOP_K):
        def wait(k, carry, choice=choice):
            row_copy(k, choice).wait()
            return carry

        lax.fori_loop(0, tm, wait, 0)
    wt = wt_ref[...]
    o_ref[...] = x_ref[...] + wt[:, 0:1] * buf[0] + wt[:, 1:2] * buf[1]


def moe_combine(x, y, dest_t, wts, *, tm, tok0):
    m, k = x.shape
    n_tok = dest_t.shape[0] // TOP_K
    row = pl.BlockSpec((tm, k), lambda i, dest: (i, 0))
    return pl.pallas_call(
        functools.partial(_combine_kernel, tm=tm, n_tok=n_tok, tok0=tok0),
        out_shape=jax.ShapeDtypeStruct((m, k), F32),
        grid_spec=pltpu.PrefetchScalarGridSpec(
            num_scalar_prefetch=1,
            grid=(m // tm,),
            in_specs=[row, pl.BlockSpec((tm, LANES), lambda i, dest: (i, 0)), pl.BlockSpec(memory_space=pl.ANY)],
            out_specs=row,
            scratch_shapes=[pltpu.VMEM((TOP_K, tm, k), F32), pltpu.SemaphoreType.DMA((TOP_K,))]),
        compiler_params=_params(("arbitrary",)),
        name="moe_combine",
    )(dest_t, x, wts, y)


def _routing_tables(ids, *, tm):
    n_assign = ids.shape[0] * TOP_K
    n_tiles = (n_assign + N_EXPERTS * (tm - 1)) // tm
    flat = ids.reshape(-1)
    onehot = (flat[:, None] == jnp.arange(N_EXPERTS, dtype=jnp.int32)[None, :]).astype(jnp.int32)
    counts = jnp.sum(onehot, axis=0)
    rank = jnp.take_along_axis(jnp.cumsum(onehot, axis=0), flat[:, None], axis=1)[:, 0] - 1
    tiles_per = (counts + tm - 1) // tm
    tile_end = jnp.cumsum(tiles_per)
    row_start = (tile_end - tiles_per) * tm
    dest = row_start[flat] + rank
    n_used = tile_end[-1]
    tile_ids = jnp.arange(n_tiles, dtype=jnp.int32)
    tile_expert = jnp.minimum(jnp.searchsorted(tile_end, tile_ids, side="right"), N_EXPERTS - 1).astype(jnp.int32)
    last_used = tile_expert[jnp.maximum(n_used - 1, 0)]
    tile_expert = jnp.where(tile_ids < n_used, tile_expert, last_used)
    src = jnp.zeros((n_tiles * tm,), jnp.int32).at[dest].set(jnp.arange(n_assign, dtype=jnp.int32) // TOP_K)
    return src, dest.reshape(-1, TOP_K), tile_expert, n_used.reshape(1).astype(jnp.int32)


def hier_moe_layer(xp, xs, layer, norm_ffn, router_w, router_b, w_gate, w_up, w_down, *, tm_e=512, fc=256):
    idp, wtp = moe_router(xp, norm_ffn[layer], router_w, router_b, tm=512)
    ids_, wts_ = moe_router(xs, norm_ffn[layer], router_w, router_b, tm=xs.shape[0])
    ids = jnp.concatenate([idp[:, :TOP_K], ids_[:, :TOP_K]], axis=0)
    src, dest, tile_expert, n_used = _routing_tables(ids, tm=tm_e)
    x_sorted = moe_dispatch(jnp.concatenate([xp, xs], axis=0), norm_ffn[layer], src, n_used * tm_e, tm=256)
    y = moe_experts(x_sorted, tile_expert, n_used, w_gate, w_up, w_down, layer=layer, tm=tm_e, fc=fc)
    dest_t = dest.T.reshape(-1)
    xp = moe_combine(xp, y, dest_t, wtp, tm=256, tok0=0)
    xs = moe_combine(xs, y, dest_t, wts_, tm=xs.shape[0], tok0=xp.shape[0])
    return xp, xs


def _pad_cols(w, n):
    return jnp.pad(w, ((0, 0), (0, n - w.shape[1])))


def kernel(x_prompt, x_sample, cache_a_k, cache_a_v, state_rwkv, state_shift, cache_sb_k, cache_sb_v, page_table,
           norm_mix, norm_ffn, norm_final, w_in0, lam_q1, lam_k1, lam_q2, lam_k2, subln0,
           rw_mu, rw_w0, rw_w2, rw_a0, rw_a2, rw_g2, rw_k_k, rw_k_a, rw_r_k, rw_ln_w, rw_ln_b, w_out0,
           w_qkv1, w_out1, router_grp_w, router_grp_b, router_exp_w, router_exp_b,
           exp_w_gate, exp_w_up, exp_w_down):
    batch, seq, d = x_prompt.shape
    s_n = x_sample.shape[0]
    n_p = batch * seq
    xp = x_prompt.reshape(n_p, d)
    xs = x_sample.reshape(s_n, d)
    n_phys = cache_a_k.shape[0]
    tm_p = 512

    lam_init = 0.8 - 0.6 * math.exp(-0.3 * 0)
    lam_rows = [z.reshape(1, A_HEAD_DIM) for z in (lam_q1, lam_k1, lam_q2, lam_k2)]
    w_q = w_in0[:, :A_WIDTH].astype(BF16)
    w_k = w_in0[:, A_WIDTH:2 * A_WIDTH].astype(BF16)
    w_v = w_in0[:, 2 * A_WIDTH:3 * A_WIDTH].astype(BF16)
    w_p = _pad_cols(w_in0[:, 3 * A_WIDTH:], B_PROJ_PAD).astype(BF16)
    g0 = norm_mix[0]

    (qp16,) = norm_matmul(xp, g0, w_q, (BF16,), tm=tm_p, tn=512)
    kp32, kp16 = norm_matmul(xp, g0, w_k, (F32, BF16), tm=tm_p, tn=512)
    vp32, vp16 = norm_matmul(xp, g0, w_v, (F32, BF16), tm=tm_p, tn=512)
    (pp,) = norm_matmul(xp, g0, w_p, (F32,), tm=tm_p, tn=1152)
    (qs32,) = norm_matmul(xs, g0, w_q, (F32,), tm=s_n, tn=512)
    (ks32,) = norm_matmul(xs, g0, w_k, (F32,), tm=s_n, tn=512)
    (vs32,) = norm_matmul(xs, g0, w_v, (F32,), tm=s_n, tn=512)
    (ps,) = norm_matmul(xs, g0, w_p, (F32,), tm=s_n, tn=1152)

    oa_p = diff_attn_prompt(qp16, kp16, vp16, lam_rows, subln0, batch=batch, seq=seq, lam_init=lam_init)
    oa_s = diff_attn_sample(qs32, ks32, vs32, cache_a_k.reshape(n_phys, PAGE_SIZE * A_HEADS, A_QK),
                            cache_a_v.reshape(n_phys, PAGE_SIZE * A_HEADS, A_QK), page_table, lam_rows, subln0,
                            lam_init=lam_init)

    zeros_tail = jnp.zeros((B_TAIL - B_DECAY_LORA - B_A_LORA - B_G_LORA, B_WIDTH), F32)
    w2_pad = jnp.concatenate([rw_w2, jnp.zeros((B_A_LORA + B_G_LORA, B_WIDTH), F32), zeros_tail], axis=0)
    a2_pad = jnp.concatenate([jnp.zeros((B_DECAY_LORA, B_WIDTH), F32), rw_a2,
                              jnp.zeros((B_G_LORA, B_WIDTH), F32), zeros_tail], axis=0)
    g2_pad = jnp.concatenate([jnp.zeros((B_DECAY_LORA + B_A_LORA, B_WIDTH), F32), rw_g2, zeros_tail], axis=0)
    row = lambda z: z.reshape(1, -1)
    prep_w = (row(jnp.pad(rw_mu, (0, B_PROJ_PAD - B_PROJ))), row(rw_w0), row(rw_a0), row(rw_k_k), row(rw_k_a),
              row(rw_r_k), w2_pad, a2_pad, g2_pad)

    r_n, d_n, k_n, na_n, b_n, v_n, g_n, bonus_n = rwkv_prep(
        pp, jnp.zeros((batch, 1, B_PROJ_PAD), F32), prep_w, batch=batch, seq=seq, tm=256, sequential=True)
    y_p, rwkv_p = rwkv_scan(na_n, r_n, d_n, b_n, k_n, v_n)
    ob_p = rwkv_post(y_p.reshape(n_p, B_WIDTH), bonus_n.reshape(n_p, B_WIDTH), g_n.reshape(n_p, B_WIDTH),
                     rw_ln_w, rw_ln_b, tm=tm_p)

    prev_s = jnp.pad(state_shift.reshape(s_n, B_PROJ), ((0, 0), (0, B_PROJ_PAD - B_PROJ)))
    r_s, d_s, k_s, na_s, b_s, v_s, g_s, bonus_s = (z.reshape(s_n, B_WIDTH) for z in rwkv_prep(
        ps, prev_s, prep_w, batch=s_n, seq=1, tm=s_n, sequential=False))
    y_s, rwkv_s = rwkv_step(na_s, r_s, d_s, b_s, k_s, v_s, state_rwkv)
    ob_s = rwkv_post(y_s, bonus_s, g_s, rw_ln_w, rw_ln_b, tm=s_n)

    w_o0 = w_out0.astype(BF16)
    xp = matmul_residual([oa_p, ob_p], w_o0, xp, tm=tm_p, tn=512)
    xs = matmul_residual([oa_s, ob_s], w_o0, xs, tm=s_n, tn=512)

    shift_p = pp.reshape(batch, seq, B_PROJ_PAD)[:, seq - 1:, :B_PROJ]
    shift_s = ps[:, :B_PROJ].reshape(s_n, 1, B_PROJ)
    a_k_p = kp32.reshape(batch, seq, A_HEADS, A_QK)
    a_v_p = vp32.reshape(batch, seq, A_HEADS, A_QK)
    a_k_s = ks32.reshape(s_n, 1, A_HEADS, A_QK)
    a_v_s = vs32.reshape(s_n, 1, A_HEADS, A_QK)

    def router_weights(layer):
        w = jnp.concatenate([router_grp_w[layer], router_exp_w[layer]], axis=1)
        b = jnp.concatenate([router_grp_b[layer], router_exp_b[layer]], axis=0)
        return _pad_cols(w, LANES), jnp.pad(b, (0, LANES - b.shape[0])).reshape(1, LANES)

    rw0, rb0 = router_weights(0)
    xp, xs = hier_moe_layer(xp, xs, 0, norm_ffn, rw0, rb0, exp_w_gate, exp_w_up, exp_w_down)

    g1 = norm_mix[1]
    w_q1 = w_qkv1[:, :d].astype(BF16)
    w_k1 = w_qkv1[:, d:2 * d].astype(BF16)
    w_v1 = w_qkv1[:, 2 * d:].astype(BF16)
    (qp16,) = norm_matmul(xp, g1, w_q1, (BF16,), tm=tm_p, tn=512)
    kp32, kp16 = norm_matmul(xp, g1, w_k1, (F32, BF16), tm=tm_p, tn=512)
    vp32, vp16 = norm_matmul(xp, g1, w_v1, (F32, BF16), tm=tm_p, tn=512)
    (qs32,) = norm_matmul(xs, g1, w_q1, (F32,), tm=s_n, tn=512)
    (ks32,) = norm_matmul(xs, g1, w_k1, (F32,), tm=s_n, tn=512)
    (vs32,) = norm_matmul(xs, g1, w_v1, (F32,), tm=s_n, tn=512)

    oc_p = sb_attn_prompt(qp16, kp16, vp16, batch=batch, seq=seq)
    oc_s = sb_attn_sample(qs32, cache_sb_k.reshape(n_phys, PAGE_SIZE * C_HEADS, C_HEAD_DIM),
                          cache_sb_v.reshape(n_phys, PAGE_SIZE * C_HEADS, C_HEAD_DIM), page_table)
    w_o1 = w_out1.astype(BF16)
    xp = matmul_residual([oc_p], w_o1, xp, tm=tm_p, tn=512)
    xs = matmul_residual([oc_s], w_o1, xs, tm=s_n, tn=512)

    sb_k_p = kp32.reshape(batch, seq, C_HEADS, C_HEAD_DIM)
    sb_v_p = vp32.reshape(batch, seq, C_HEADS, C_HEAD_DIM)
    sb_k_s = ks32.reshape(s_n, 1, C_HEADS, C_HEAD_DIM)
    sb_v_s = vs32.reshape(s_n, 1, C_HEADS, C_HEAD_DIM)

    rw1, rb1 = router_weights(1)
    xp, xs = hier_moe_layer(xp, xs, 1, norm_ffn, rw1, rb1, exp_w_gate, exp_w_up, exp_w_down)

    y_prompt = rmsnorm_rows(xp, norm_final, tm=tm_p).reshape(batch, seq, d)
    y_sample = rmsnorm_rows(xs, norm_final, tm=s_n).reshape(s_n, 1, d)
    return (y_prompt, y_sample, a_k_p, a_v_p, a_k_s, a_v_s, rwkv_p, rwkv_s, shift_p, shift_s,
            sb_k_p, sb_v_p, sb_k_s, sb_v_s)
```

```python
import functools
import math

import jax
import jax.numpy as jnp
from jax import lax
from jax.experimental import pallas as pl
from jax.experimental.pallas import tpu as pltpu

F32 = jnp.float32
BF16 = jnp.bfloat16
HIGHEST = lax.Precision.HIGHEST

D_MODEL = 2048
A_WIDTH = 1024
A_HEAD_DIM = 64
A_HEADS = 8
A_QK = 128
B_WIDTH = 1024
B_HEAD = 64
B_HEADS = 16
B_DECAY_LORA = 64
B_A_LORA = 64
B_G_LORA = 160
B_PROJ = 3 * B_WIDTH + B_DECAY_LORA + B_A_LORA + B_G_LORA
B_PROJ_PAD = 3456
B_TAIL = B_PROJ_PAD - 3 * B_WIDTH
C_HEADS = 16
C_HEAD_DIM = 128
N_GROUPS = 4
EXPERTS_PER_GROUP = 8
N_EXPERTS = 32
TOP_K = 2
EXPERT_HIDDEN = 1024
PAGE_SIZE = 128
RMS_EPS = 1e-6
GN_EPS = 64e-5
NEG_BIG = -1e30

LANES = 128
SUBLANES = 8
V7X_VMEM_BYTES = 64 * 1024 * 1024
VMEM_LIMIT = 48 * 1024 * 1024
DMA_LOOP_UNROLL = 8
PAGE_BUFFERS = 4


def _params(sem, vmem=VMEM_LIMIT):
    return pltpu.CompilerParams(dimension_semantics=sem, vmem_limit_bytes=vmem)


def _dot(a, b):
    return jnp.dot(a, b, preferred_element_type=F32)


def _dot_nt(a, b):
    return lax.dot_general(a, b, (((1,), (1,)), ((), ())), preferred_element_type=F32)


def _dot_f32(a, b):
    return jnp.dot(a, b, preferred_element_type=F32, precision=HIGHEST)


def _norm_matmul_kernel(x_ref, g_ref, w_ref, *refs, n_out):
    outs, h_scr = refs[:n_out], refs[n_out]

    @pl.when(pl.program_id(1) == 0)
    def _():
        x = x_ref[...]
        y = x * lax.rsqrt(jnp.mean(x * x, axis=-1, keepdims=True) + RMS_EPS)
        h_scr[...] = (y * g_ref[...]).astype(BF16)

    acc = _dot(h_scr[...], w_ref[...])
    for o in outs:
        o[...] = acc.astype(o.dtype)


def norm_matmul(x, gain, w, out_dtypes, *, tm, tn):
    m, k = x.shape
    n = w.shape[1]
    assert m % tm == 0 and n % tn == 0
    outs = tuple(jax.ShapeDtypeStruct((m, n), dt) for dt in out_dtypes)
    return pl.pallas_call(
        functools.partial(_norm_matmul_kernel, n_out=len(outs)),
        out_shape=outs,
        grid=(m // tm, n // tn),
        in_specs=[pl.BlockSpec((tm, k), lambda i, j: (i, 0)),
                  pl.BlockSpec((1, k), lambda i, j: (0, 0)),
                  pl.BlockSpec((k, tn), lambda i, j: (0, j))],
        out_specs=tuple(pl.BlockSpec((tm, tn), lambda i, j: (i, j)) for _ in outs),
        scratch_shapes=[pltpu.VMEM((tm, k), BF16)],
        compiler_params=_params(("parallel", "arbitrary")),
        name="norm_matmul",
    )(x, gain.reshape(1, k), w)


def _matmul_res_kernel(*refs, n_a):
    a_refs, w_refs, r_ref, o_ref = refs[:n_a], refs[n_a:2 * n_a], refs[2 * n_a], refs[2 * n_a + 1]
    acc = r_ref[...]
    for a, w in zip(a_refs, w_refs):
        acc = acc + _dot(a[...], w[...])
    o_ref[...] = acc


def matmul_residual(a_list, w, resid, *, tm, tn):
    m, n = resid.shape
    n_a = len(a_list)
    kk = a_list[0].shape[1]
    assert all(a.shape == (m, kk) for a in a_list) and w.shape == (n_a * kk, n)
    in_specs = [pl.BlockSpec((tm, kk), lambda i, j: (i, 0)) for _ in a_list]
    in_specs += [pl.BlockSpec((kk, tn), lambda i, j, c=c: (c, j)) for c in range(n_a)]
    in_specs += [pl.BlockSpec((tm, tn), lambda i, j: (i, j))]
    return pl.pallas_call(
        functools.partial(_matmul_res_kernel, n_a=n_a),
        out_shape=jax.ShapeDtypeStruct((m, n), F32),
        grid=(m // tm, n // tn),
        in_specs=in_specs,
        out_specs=pl.BlockSpec((tm, tn), lambda i, j: (i, j)),
        compiler_params=_params(("parallel", "parallel")),
        name="matmul_residual",
    )(*a_list, *([w] * n_a), resid)


def _rmsnorm_kernel(x_ref, g_ref, o_ref):
    x = x_ref[...]
    y = x * lax.rsqrt(jnp.mean(x * x, axis=-1, keepdims=True) + RMS_EPS)
    o_ref[...] = y * g_ref[...]


def rmsnorm_rows(x, gain, *, tm):
    m, k = x.shape
    return pl.pallas_call(
        _rmsnorm_kernel,
        out_shape=jax.ShapeDtypeStruct((m, k), F32),
        grid=(m // tm,),
        in_specs=[pl.BlockSpec((tm, k), lambda i: (i, 0)), pl.BlockSpec((1, k), lambda i: (0, 0))],
        out_specs=pl.BlockSpec((tm, k), lambda i: (i, 0)),
        compiler_params=_params(("parallel",)),
        name="final_rmsnorm",
    )(x, gain.reshape(1, k))


def _lam_value(lq1, lk1, lq2, lk2, lam_init):
    s1 = jnp.sum(lq1[...] * lk1[...], axis=-1, keepdims=True)
    s2 = jnp.sum(lq2[...] * lk2[...], axis=-1, keepdims=True)
    return jnp.exp(s1) - jnp.exp(s2) + lam_init


def _subln(o, sub_ref, lam_init):
    y = o * lax.rsqrt(jnp.mean(o * o, axis=-1, keepdims=True) + RMS_EPS)
    return (y * sub_ref[...]) * (1.0 - lam_init)


def _softmax_step(s, m, l, acc, v):
    m_new = jnp.maximum(m, jnp.max(s, axis=-1, keepdims=True))
    alpha = jnp.exp(m - m_new)
    p = jnp.exp(s - m_new)
    l = alpha * l + jnp.sum(p, axis=-1, keepdims=True)
    acc = alpha * acc + _dot(p.astype(BF16), v)
    return m_new, l, acc


def _diff_attn_prompt_kernel(q_ref, k_ref, v_ref, lq1, lk1, lq2, lk2, sub_ref, o_ref, *, tq, lam_init):
    h = pl.program_id(1)
    i = pl.program_id(2)
    scale = A_HEAD_DIM ** -0.5
    q = q_ref[...]
    lane = lax.broadcasted_iota(jnp.int32, (tq, A_QK), 1)
    q1 = jnp.where(lane < A_HEAD_DIM, q, jnp.zeros_like(q))
    q2 = jnp.where(lane >= A_HEAD_DIM, q, jnp.zeros_like(q))
    hh = (h + 1).astype(F32) * (8.0 / A_HEADS)
    slope = jnp.exp2(jnp.zeros((1, tq), F32) - hh)
    qpos = (i * tq).astype(F32) + lax.broadcasted_iota(jnp.int32, (tq, 1), 0).astype(F32)
    kiota = lax.broadcasted_iota(jnp.int32, (1, tq), 1).astype(F32)

    def tile(j, carry, masked):
        m1, l1, a1, m2, l2, a2 = carry
        start = pl.multiple_of(j * tq, tq)
        kj = k_ref[pl.ds(start, tq), :]
        vj = v_ref[pl.ds(start, tq), :]
        dist = qpos - ((j * tq).astype(F32) + kiota)
        bias = slope * dist
        s1 = _dot_nt(q1, kj) * scale - bias
        s2 = _dot_nt(q2, kj) * scale - bias
        if masked:
            ok = dist >= 0.0
            s1 = jnp.where(ok, s1, NEG_BIG)
            s2 = jnp.where(ok, s2, NEG_BIG)
        m1, l1, a1 = _softmax_step(s1, m1, l1, a1, vj)
        m2, l2, a2 = _softmax_step(s2, m2, l2, a2, vj)
        return m1, l1, a1, m2, l2, a2

    z1 = jnp.zeros((tq, 1), F32)
    za = jnp.zeros((tq, A_QK), F32)
    init = (z1 + NEG_BIG, z1, za, z1 + NEG_BIG, z1, za)
    carry = lax.fori_loop(0, i, lambda j, c: tile(j, c, False), init)
    m1, l1, a1, m2, l2, a2 = tile(i, carry, True)
    lam = _lam_value(lq1, lk1, lq2, lk2, lam_init)
    o = a1 / l1 - lam * (a2 / l2)
    o_ref[...] = _subln(o, sub_ref, lam_init).astype(o_ref.dtype)


def diff_attn_prompt(q, k, v, lam_rows, subln, *, batch, seq, lam_init, tq=512):
    nq = seq // tq
    small = [pl.BlockSpec((1, A_HEAD_DIM), lambda b, h, i: (0, 0)) for _ in range(4)]
    return pl.pallas_call(
        functools.partial(_diff_attn_prompt_kernel, tq=tq, lam_init=lam_init),
        out_shape=jax.ShapeDtypeStruct((batch * seq, A_WIDTH), BF16),
        grid=(batch, A_HEADS, nq),
        in_specs=[pl.BlockSpec((tq, A_QK), lambda b, h, i: (b * nq + i, h)),
                  pl.BlockSpec((seq, A_QK), lambda b, h, i: (b, h)),
                  pl.BlockSpec((seq, A_QK), lambda b, h, i: (b, h))] + small
                 + [pl.BlockSpec((1, A_QK), lambda b, h, i: (0, 0))],
        out_specs=pl.BlockSpec((tq, A_QK), lambda b, h, i: (b * nq + i, h)),
        compiler_params=_params(("parallel", "parallel", "arbitrary")),
        name="diff_attn_prompt",
    )(q, k, v, *lam_rows, subln.reshape(1, A_QK))


def _page_pipeline(pt_ref, k_hbm, v_hbm, kbuf, vbuf, sem, *, n_steps, pps, page_of):
    t = pl.program_id(0) * n_steps + pl.program_id(1)
    total = pl.num_programs(0) * n_steps
    depth = PAGE_BUFFERS - 1

    def copies(tt):
        seq, step, slot = tt // n_steps, tt % n_steps, tt % PAGE_BUFFERS
        out = []
        for n in range(pps):
            page = pt_ref[seq, page_of(step, n)]
            out.append(pltpu.make_async_copy(k_hbm.at[page], kbuf.at[slot, n], sem.at[slot, 0, n]))
            out.append(pltpu.make_async_copy(v_hbm.at[page], vbuf.at[slot, n], sem.at[slot, 1, n]))
        return out

    @pl.when(t == 0)
    def _():
        for d in range(depth):
            for c in copies(d):
                c.start()

    @pl.when(t + depth < total)
    def _():
        for c in copies(t + depth):
            c.start()

    for c in copies(t):
        c.wait()
    return t % PAGE_BUFFERS


def _page_pipeline_scratch(pps, n_keys, width):
    return [pltpu.VMEM((PAGE_BUFFERS, pps, n_keys, width), F32), pltpu.VMEM((PAGE_BUFFERS, pps, n_keys, width), F32),
            pltpu.SemaphoreType.DMA((PAGE_BUFFERS, 2, pps))]


def _diff_attn_sample_kernel(pt_ref, q_ref, kn_ref, vn_ref, k_hbm, v_hbm, lq1, lk1, lq2, lk2, sub_ref,
                             o_ref, m_scr, l_scr, acc_scr, kbuf, vbuf, sem, *, n_pages, pps, lam_init):
    slot = _page_pipeline(pt_ref, k_hbm, v_hbm, kbuf, vbuf, sem, n_steps=n_pages // pps, pps=pps,
                          page_of=lambda step, n: step * pps + n)
    p = pl.program_id(1)
    rows = 2 * A_HEADS
    n_keys = PAGE_SIZE * A_HEADS * pps
    scale = A_HEAD_DIM ** -0.5

    @pl.when(p == 0)
    def _():
        m_scr[...] = jnp.full((rows, 1), NEG_BIG, F32)
        l_scr[...] = jnp.zeros((rows, 1), F32)
        acc_scr[...] = jnp.zeros((rows, A_QK), F32)

    q = q_ref[0]
    lane = lax.broadcasted_iota(jnp.int32, (A_HEADS, A_QK), 1)
    qm = jnp.concatenate([jnp.where(lane < A_HEAD_DIM, q, 0.0), jnp.where(lane >= A_HEAD_DIM, q, 0.0)], axis=0)
    row = lax.broadcasted_iota(jnp.int32, (rows, n_keys), 0)
    col = lax.broadcasted_iota(jnp.int32, (rows, n_keys), 1)
    own = (col % A_HEADS) == (row % A_HEADS)
    head1 = (lax.broadcasted_iota(jnp.int32, (rows, 1), 0) % A_HEADS + 1).astype(F32) * (8.0 / A_HEADS)
    slope = jnp.exp2(-head1)
    kpos = (p * (PAGE_SIZE * pps)).astype(F32) + (col // A_HEADS).astype(F32)
    dist = float(n_pages * PAGE_SIZE) - kpos
    qb = qm.astype(BF16)
    s = jnp.concatenate([_dot_nt(qb, kbuf[slot, n].astype(BF16)) for n in range(pps)], axis=1)
    s = jnp.where(own, s * scale - slope * dist, NEG_BIG)
    m0, l0 = m_scr[...], l_scr[...]
    m = jnp.maximum(m0, jnp.max(s, axis=-1, keepdims=True))
    alpha = jnp.exp(m0 - m)
    pr = jnp.exp(s - m)
    l = alpha * l0 + jnp.sum(pr, axis=-1, keepdims=True)
    acc = alpha * acc_scr[...]
    page_keys = PAGE_SIZE * A_HEADS
    for n in range(pps):
        acc = acc + _dot(pr[:, n * page_keys:(n + 1) * page_keys].astype(BF16), vbuf[slot, n].astype(BF16))
    m_scr[...] = m
    l_scr[...] = l
    acc_scr[...] = acc

    @pl.when(p == n_pages // pps - 1)
    def _():
        kn2 = jnp.concatenate([kn_ref[0], kn_ref[0]], axis=0)
        vn2 = jnp.concatenate([vn_ref[0], vn_ref[0]], axis=0)
        s_new = jnp.sum(qm * kn2, axis=-1, keepdims=True) * scale
        m_new = jnp.maximum(m, s_new)
        alpha = jnp.exp(m - m_new)
        p_new = jnp.exp(s_new - m_new)
        o16 = (alpha * acc + p_new * vn2) / (alpha * l + p_new)
        lam = _lam_value(lq1, lk1, lq2, lk2, lam_init)
        o = o16[:A_HEADS] - lam * o16[A_HEADS:]
        o_ref[0] = _subln(o, sub_ref, lam_init).astype(o_ref.dtype)


def diff_attn_sample(q, k_new, v_new, cache_k, cache_v, page_table, lam_rows, subln, *, lam_init):
    s_n, n_pages = page_table.shape
    pps = 2
    assert n_pages % pps == 0
    n_keys = PAGE_SIZE * A_HEADS
    small = [pl.BlockSpec((1, A_HEAD_DIM), lambda s, p, pt: (0, 0)) for _ in range(4)]
    row_spec = pl.BlockSpec((1, A_HEADS, A_QK), lambda s, p, pt: (s, 0, 0))
    hbm = pl.BlockSpec(memory_space=pl.ANY)
    rows = 2 * A_HEADS
    heads = lambda z: z.reshape(s_n, A_HEADS, A_QK)
    out = pl.pallas_call(
        functools.partial(_diff_attn_sample_kernel, n_pages=n_pages, pps=pps, lam_init=lam_init),
        out_shape=jax.ShapeDtypeStruct((s_n, A_HEADS, A_QK), BF16),
        grid_spec=pltpu.PrefetchScalarGridSpec(
            num_scalar_prefetch=1,
            grid=(s_n, n_pages // pps),
            in_specs=[row_spec, row_spec, row_spec, hbm, hbm] + small
                     + [pl.BlockSpec((1, A_QK), lambda s, p, pt: (0, 0))],
            out_specs=pl.BlockSpec((1, A_HEADS, A_QK), lambda s, p, pt: (s, 0, 0)),
            scratch_shapes=[pltpu.VMEM((rows, 1), F32), pltpu.VMEM((rows, 1), F32),
                            pltpu.VMEM((rows, A_QK), F32)] + _page_pipeline_scratch(pps, n_keys, A_QK)),
        compiler_params=_params(("arbitrary", "arbitrary")),
        name="diff_attn_sample",
    )(page_table, heads(q), heads(k_new), heads(v_new), cache_k, cache_v, *lam_rows, subln.reshape(1, A_QK))
    return out.reshape(s_n, A_WIDTH)


def _log_sigmoid(z):
    return jnp.minimum(z, 0.0) - jnp.log1p(jnp.exp(-jnp.abs(z)))


def _suffix_sum(lk, tri):
    hi = lk.astype(BF16)
    lo = (lk - hi.astype(F32)).astype(BF16)
    return _dot(hi, tri) + _dot(lo, tri)


def _strict_lower_ones(n):
    r = lax.broadcasted_iota(jnp.int32, (n, n), 0)
    c = lax.broadcasted_iota(jnp.int32, (n, n), 1)
    return jnp.where(r > c, 1.0, 0.0).astype(BF16)


def _sb_prompt_kernel(q_ref, k_ref, v_ref, o_ref, *, tq, hpg):
    i = pl.program_id(2)
    scale = C_HEAD_DIM ** -0.5
    tri = _strict_lower_ones(tq)
    rr = lax.broadcasted_iota(jnp.int32, (tq, tq), 0)
    cc = lax.broadcasted_iota(jnp.int32, (tq, tq), 1)
    before = cc < rr
    qs = [q_ref[:, h * C_HEAD_DIM:(h + 1) * C_HEAD_DIM] for h in range(hpg)]

    def tile(j, carry, masked):
        start = pl.multiple_of(j * tq, tq)
        out = []
        for h in range(hpg):
            c, acc = carry[h]
            lanes = slice(h * C_HEAD_DIM, (h + 1) * C_HEAD_DIM)
            kj = k_ref[pl.ds(start, tq), lanes]
            vj = v_ref[pl.ds(start, tq), lanes]
            z = _dot_nt(qs[h], kj) * scale
            ls = _log_sigmoid(z)
            lk = ls - z
            if masked:
                lk = jnp.where(before, lk, 0.0)
            later = c + _suffix_sum(lk, tri)
            att = jnp.exp(ls + later)
            if masked:
                att = jnp.where(before, att, 0.0)
            acc = acc + _dot(att.astype(BF16), vj)
            c = c + jnp.sum(lk, axis=-1, keepdims=True)
            out.append((c, acc))
        return tuple(out)

    zero = (jnp.zeros((tq, 1), F32), jnp.zeros((tq, C_HEAD_DIM), F32))
    carry = tile(i, (zero,) * hpg, True)
    carry = lax.fori_loop(0, i, lambda jj, c: tile(i - 1 - jj, c, False), carry)
    o_ref[...] = jnp.concatenate([acc for _, acc in carry], axis=1).astype(o_ref.dtype)


def sb_attn_prompt(q, k, v, *, batch, seq, tq=512, hpg=2):
    nq = seq // tq
    width = C_HEADS * C_HEAD_DIM
    wl = hpg * C_HEAD_DIM
    return pl.pallas_call(
        functools.partial(_sb_prompt_kernel, tq=tq, hpg=hpg),
        out_shape=jax.ShapeDtypeStruct((batch * seq, width), BF16),
        grid=(batch, C_HEADS // hpg, nq),
        in_specs=[pl.BlockSpec((tq, wl), lambda b, h, i: (b * nq + i, h)),
                  pl.BlockSpec((seq, wl), lambda b, h, i: (b, h)),
                  pl.BlockSpec((seq, wl), lambda b, h, i: (b, h))],
        out_specs=pl.BlockSpec((tq, wl), lambda b, h, i: (b * nq + i, h)),
        compiler_params=_params(("parallel", "parallel", "arbitrary")),
        name="sb_attn_prompt",
    )(q, k, v)


def _sb_sample_kernel(pt_ref, q_ref, k_hbm, v_hbm, o_ref, c_scr, acc_scr, kbuf, vbuf, sem, *, n_pages):
    p = pl.program_id(1)
    n_tiles = PAGE_SIZE * C_HEADS // LANES
    scale = C_HEAD_DIM ** -0.5

    @pl.when(p == 0)
    def _():
        c_scr[...] = jnp.zeros((1, LANES), F32)
        acc_scr[...] = jnp.zeros((C_HEADS, C_HEAD_DIM), F32)

    slot = _page_pipeline(pt_ref, k_hbm, v_hbm, kbuf, vbuf, sem, n_steps=n_pages, pps=1,
                          page_of=lambda step, n: n_pages - 1 - step)
    kf = kbuf[slot, 0].astype(BF16)
    vf = vbuf[slot, 0].astype(BF16)
    zt = _dot_nt(q_ref[0].astype(BF16), kf)
    row = lax.broadcasted_iota(jnp.int32, (C_HEADS, LANES), 0)
    lane = lax.broadcasted_iota(jnp.int32, (C_HEADS, LANES), 1)
    own = (lane % C_HEADS) == row
    z = jnp.concatenate([jnp.sum(jnp.where(own, zt[:, g * LANES:(g + 1) * LANES], 0.0), axis=0, keepdims=True)
                         for g in range(n_tiles)], axis=0) * scale
    ls = _log_sigmoid(z)
    lk = ls - z
    li = lax.broadcasted_iota(jnp.int32, (LANES, LANES), 0)
    lj = lax.broadcasted_iota(jnp.int32, (LANES, LANES), 1)
    same = (li % C_HEADS) == (lj % C_HEADS)
    hi = lk.astype(BF16)
    lo = (lk - hi.astype(F32)).astype(BF16)
    sel = jnp.concatenate([jnp.where(same & (li > lj), 1.0, 0.0), jnp.where(same, 1.0, 0.0)], axis=1).astype(BF16)
    both = _dot(hi, sel) + _dot(lo, sel)
    inside, total = both[:, :LANES], both[:, LANES:]
    run = c_scr[...]
    later = [None] * n_tiles
    for g in range(n_tiles - 1, -1, -1):
        later[g] = run + inside[g:g + 1]
        run = run + total[g:g + 1]
    c_scr[...] = run
    att = jnp.exp(ls + jnp.concatenate(later, axis=0))
    spread = jnp.concatenate([jnp.where(own, att[g:g + 1], 0.0) for g in range(n_tiles)], axis=1)
    acc = acc_scr[...] + _dot(spread.astype(BF16), vf)
    acc_scr[...] = acc

    @pl.when(p == pl.num_programs(1) - 1)
    def _():
        o_ref[0] = acc.astype(o_ref.dtype)


def sb_attn_sample(q, cache_k, cache_v, page_table):
    s_n, n_pages = page_table.shape
    n_keys = PAGE_SIZE * C_HEADS
    hbm = pl.BlockSpec(memory_space=pl.ANY)
    out = pl.pallas_call(
        functools.partial(_sb_sample_kernel, n_pages=n_pages),
        out_shape=jax.ShapeDtypeStruct((s_n, C_HEADS, C_HEAD_DIM), BF16),
        grid_spec=pltpu.PrefetchScalarGridSpec(
            num_scalar_prefetch=1,
            grid=(s_n, n_pages),
            in_specs=[pl.BlockSpec((1, C_HEADS, C_HEAD_DIM), lambda s, p, pt: (s, 0, 0)), hbm, hbm],
            out_specs=pl.BlockSpec((1, C_HEADS, C_HEAD_DIM), lambda s, p, pt: (s, 0, 0)),
            scratch_shapes=[pltpu.VMEM((1, LANES), F32), pltpu.VMEM((C_HEADS, C_HEAD_DIM), F32)]
                           + _page_pipeline_scratch(1, n_keys, C_HEAD_DIM)),
        compiler_params=_params(("arbitrary", "arbitrary")),
        name="sb_attn_sample",
    )(page_table, q.reshape(s_n, C_HEADS, C_HEAD_DIM), cache_k, cache_v)
    return out.reshape(s_n, C_HEADS * C_HEAD_DIM)


def _head_sum(x):
    r = lax.broadcasted_iota(jnp.int32, (LANES, LANES), 0) // B_HEAD
    c = lax.broadcasted_iota(jnp.int32, (LANES, LANES), 1) // B_HEAD
    ones = jnp.where(r == c, 1.0, 0.0).astype(F32)
    parts = [_dot_f32(x[:, g * LANES:(g + 1) * LANES], ones) for g in range(x.shape[1] // LANES)]
    return jnp.concatenate(parts, axis=1)


def _softplus(x):
    return jnp.maximum(x, 0.0) + jnp.log1p(jnp.exp(-jnp.abs(x)))


def _sigmoid(x):
    return 1.0 / (1.0 + jnp.exp(-x))


def _rwkv_prep_kernel(p_ref, prev_ref, mu_ref, w0_ref, a0_ref, kk_ref, ka_ref, rk_ref, w2_ref, a2_ref, g2_ref,
                      *refs, sequential):
    outs = refs[:10]
    p = p_ref[...]
    tm = p.shape[0]
    if sequential:
        last_scr = refs[10]
        t = pl.program_id(1)

        @pl.when(t == 0)
        def _():
            last_scr[...] = prev_ref[0]

        rolled = pltpu.roll(p, 1, axis=0)
        row = lax.broadcasted_iota(jnp.int32, (tm, 1), 0)
        shifted = jnp.where(row == 0, last_scr[...], rolled)
        last_scr[...] = p[tm - 1:tm, :]
    else:
        shifted = prev_ref[...]
    xs = p + (shifted - p) * mu_ref[...]
    r = xs[:, :B_WIDTH]
    k = xs[:, B_WIDTH:2 * B_WIDTH]
    v = xs[:, 2 * B_WIDTH:3 * B_WIDTH]
    tail = xs[:, 3 * B_WIDTH:]
    w = -_softplus(-(w0_ref[...] + _dot_f32(jnp.tanh(tail), w2_ref[...]))) - 0.5
    decay = jnp.exp(-jnp.exp(w))
    a = _sigmoid(a0_ref[...] + _dot_f32(tail, a2_ref[...]))
    g = _dot_f32(_sigmoid(tail), g2_ref[...])
    kk = k * kk_ref[...]
    kk = kk * lax.rsqrt(jnp.maximum(_head_sum(kk * kk), 1e-24))
    k_h = k * (1.0 + (a - 1.0) * ka_ref[...])
    bonus = _head_sum(r * k_h * rk_ref[...]) * v
    b = kk * a
    r_o, d_o, k_o, na_o, b_o, v_o, g_o, bonus_o, yw_o, vkr_o = outs
    r_o[0] = r
    d_o[0] = decay
    k_o[0] = k_h
    na_o[0] = -kk
    b_o[0] = b
    v_o[0] = v
    g_o[0] = g
    bonus_o[0] = bonus
    yw_o[0] = decay * r - kk * _head_sum(b * r)
    vkr_o[0] = v * _head_sum(k_h * r)


def rwkv_prep(p, prev, weights, *, batch, seq, tm, sequential):
    nt = seq // tm if sequential else 1
    nb = batch if sequential else (batch * seq) // tm
    vec = lambda n: pl.BlockSpec((1, n), lambda b, t: (0, 0))
    mat = lambda: pl.BlockSpec((B_TAIL, B_WIDTH), lambda b, t: (0, 0))
    if sequential:
        p_spec = pl.BlockSpec((tm, B_PROJ_PAD), lambda b, t: (b * nt + t, 0))
        prev_spec = pl.BlockSpec((1, 1, B_PROJ_PAD), lambda b, t: (b, 0, 0))
        n_shape = jax.ShapeDtypeStruct((batch, seq, B_WIDTH), F32)
        n_spec = pl.BlockSpec((1, tm, B_WIDTH), lambda b, t: (b, t, 0))
        scratch = [pltpu.VMEM((1, B_PROJ_PAD), F32)]
    else:
        p_spec = pl.BlockSpec((tm, B_PROJ_PAD), lambda b, t: (b, 0))
        prev_spec = pl.BlockSpec((tm, B_PROJ_PAD), lambda b, t: (b, 0))
        n_shape = jax.ShapeDtypeStruct((nb, tm, B_WIDTH), F32)
        n_spec = pl.BlockSpec((1, tm, B_WIDTH), lambda b, t: (b, 0, 0))
        scratch = []
    return pl.pallas_call(
        functools.partial(_rwkv_prep_kernel, sequential=sequential),
        out_shape=(n_shape,) * 10,
        grid=(nb, nt),
        in_specs=[p_spec, prev_spec, vec(B_PROJ_PAD), vec(B_WIDTH), vec(B_WIDTH), vec(B_WIDTH), vec(B_WIDTH),
                  vec(B_WIDTH), mat(), mat(), mat()],
        out_specs=(n_spec,) * 10,
        scratch_shapes=scratch,
        compiler_params=_params(("parallel", "arbitrary")),
        name="rwkv_prep",
    )(p, prev, *weights)


def _rwkv_scan_kernel(a_ref, d_ref, b_ref, k_ref, v_ref, yw_ref, vkr_ref, y_ref, s_ref, st_scr, *, nb, tc, n_groups):
    t_idx = pl.program_id(1)
    n_pairs = B_HEADS // 2

    @pl.when(t_idx == 0)
    def _():
        st_scr[...] = jnp.zeros(st_scr.shape, F32)

    r_i = lax.broadcasted_iota(jnp.int32, (B_HEAD, LANES), 0)
    l_i = lax.broadcasted_iota(jnp.int32, (B_HEAD, LANES), 1)
    diag = (l_i % B_HEAD) == r_i
    kr = lax.broadcasted_iota(jnp.int32, (LANES, LANES), 0) // B_HEAD
    kc = lax.broadcasted_iota(jnp.int32, (LANES, LANES), 1) // B_HEAD
    ones_blk = jnp.where(kr == kc, 1.0, 0.0).astype(BF16)
    chains = [(bb, pr) for bb in range(nb) for pr in range(n_pairs)]
    per = len(chains) // n_groups
    groups = [chains[g * per:(g + 1) * per] for g in range(n_groups)]
    refs = dict(a=a_ref, d=d_ref, b=b_ref, k=k_ref, v=v_ref, yw=yw_ref, vkr=vkr_ref)

    def block8(blk, carry):
        t0 = pl.multiple_of(blk * SUBLANES, SUBLANES)
        rows8 = {n: [ref[bb, pl.ds(t0, SUBLANES), :] for bb in range(nb)] for n, ref in refs.items()}
        y_rows = [[[None] * n_pairs for _ in range(SUBLANES)] for _ in range(nb)]
        for cc in range(SUBLANES):
            for grp in groups:
                row = lambda n, bb, pr: rows8[n][bb][cc:cc + 1, pr * LANES:(pr + 1) * LANES]
                lhs = []
                for bb, pr in grp:
                    st = st_scr[bb * n_pairs + pr]
                    lhs.append((st * row("a", bb, pr)).astype(BF16))
                    lhs.append(jnp.where(diag, row("v", bb, pr), 0.0).astype(BF16))
                    lhs.append((st * row("yw", bb, pr) + jnp.where(diag, row("vkr", bb, pr), 0.0)).astype(BF16))
                res = _dot(jnp.concatenate(lhs, axis=0), ones_blk)
                for n, (bb, pr) in enumerate(grp):
                    base = 3 * n * B_HEAD
                    sa_b = res[base:base + B_HEAD]
                    v_b = res[base + B_HEAD:base + 2 * B_HEAD]
                    y_b = res[base + 2 * B_HEAD:base + 3 * B_HEAD]
                    st = st_scr[bb * n_pairs + pr]
                    st_scr[bb * n_pairs + pr] = st * row("d", bb, pr) + sa_b * row("b", bb, pr) + v_b * row("k", bb, pr)
                    y_rows[bb][cc][pr] = jnp.sum(jnp.where(diag, y_b, 0.0), axis=0, keepdims=True)
        for bb in range(nb):
            y8 = [jnp.concatenate(y_rows[bb][cc], axis=1) for cc in range(SUBLANES)]
            y_ref[bb, pl.ds(t0, SUBLANES), :] = jnp.concatenate(y8, axis=0)
        return carry

    lax.fori_loop(0, tc // SUBLANES, block8, 0)

    @pl.when(t_idx == pl.num_programs(1) - 1)
    def _():
        for bb, pr in chains:
            st = st_scr[bb * n_pairs + pr]
            for half in range(2):
                s_ref[bb, 2 * pr + half] = st[:, half * B_HEAD:(half + 1) * B_HEAD]


def rwkv_scan(na, d, b, k, v, yw, vkr, *, nb=2, tc=128, n_groups=1):
    batch, seq, _ = v.shape
    spec = pl.BlockSpec((nb, tc, B_WIDTH), lambda g, t: (g, t, 0))
    return pl.pallas_call(
        functools.partial(_rwkv_scan_kernel, nb=nb, tc=tc, n_groups=n_groups),
        out_shape=(jax.ShapeDtypeStruct((batch, seq, B_WIDTH), F32),
                   jax.ShapeDtypeStruct((batch, B_HEADS, B_HEAD, B_HEAD), F32)),
        grid=(batch // nb, seq // tc),
        in_specs=[spec] * 7,
        out_specs=(spec, pl.BlockSpec((nb, B_HEADS, B_HEAD, B_HEAD), lambda g, t: (g, 0, 0, 0))),
        scratch_shapes=[pltpu.VMEM((nb * B_HEADS // 2, B_HEAD, LANES), F32)],
        compiler_params=_params(("parallel", "arbitrary")),
        name="rwkv_scan",
    )(na, d, b, k, v, yw, vkr)


def _rwkv_step_kernel(a_ref, r_ref, d_ref, b_ref, k_ref, v_ref, s0_ref, y_ref, s_ref):
    r_i = lax.broadcasted_iota(jnp.int32, (B_HEAD, B_HEAD), 0)
    c_i = lax.broadcasted_iota(jnp.int32, (B_HEAD, B_HEAD), 1)
    eye = jnp.where(r_i == c_i, 1.0, 0.0).astype(F32)
    for h in range(B_HEADS):
        row = lambda ref: ref[0, h:h + 1, :]
        st = s0_ref[0, h]
        sa = jnp.sum(st * row(a_ref), axis=1, keepdims=True)
        v_col = jnp.sum(eye * row(v_ref), axis=1, keepdims=True)
        st = st * row(d_ref) + sa * row(b_ref) + v_col * row(k_ref)
        s_ref[0, h] = st
        y_col = jnp.sum(st * row(r_ref), axis=1, keepdims=True)
        y_ref[0, h:h + 1, :] = jnp.sum(eye * y_col, axis=0, keepdims=True)


def rwkv_step(na, r, d, b, k, v, s0):
    s_n = s0.shape[0]
    hs = lambda z: z.reshape(s_n, B_HEADS, B_HEAD)
    vec = pl.BlockSpec((1, B_HEADS, B_HEAD), lambda s: (s, 0, 0))
    st = pl.BlockSpec((1, B_HEADS, B_HEAD, B_HEAD), lambda s: (s, 0, 0, 0))
    y, s_new = pl.pallas_call(
        _rwkv_step_kernel,
        out_shape=(jax.ShapeDtypeStruct((s_n, B_HEADS, B_HEAD), F32),
                   jax.ShapeDtypeStruct(s0.shape, F32)),
        grid=(s_n,),
        in_specs=[vec] * 6 + [st],
        out_specs=(vec, st),
        compiler_params=_params(("parallel",)),
        name="rwkv_step",
    )(hs(na), hs(r), hs(d), hs(b), hs(k), hs(v), s0)
    return y.reshape(s_n, B_WIDTH), s_new


def _rwkv_post_kernel(y_ref, bonus_ref, g_ref, lw_ref, lb_ref, o_ref):
    y = y_ref[...]
    mean = _head_sum(y) * (1.0 / B_HEAD)
    yc = y - mean
    var = _head_sum(yc * yc) * (1.0 / B_HEAD)
    out = yc * lax.rsqrt(var + GN_EPS) * lw_ref[...] + lb_ref[...]
    o_ref[...] = ((out + bonus_ref[...]) * g_ref[...]).astype(o_ref.dtype)


def rwkv_post(y, bonus, g, ln_w, ln_b, *, tm):
    m = y.shape[0]
    row = pl.BlockSpec((tm, B_WIDTH), lambda i: (i, 0))
    vec = pl.BlockSpec((1, B_WIDTH), lambda i: (0, 0))
    return pl.pallas_call(
        _rwkv_post_kernel,
        out_shape=jax.ShapeDtypeStruct((m, B_WIDTH), BF16),
        grid=(m // tm,),
        in_specs=[row, row, row, vec, vec],
        out_specs=row,
        compiler_params=_params(("parallel",)),
        name="rwkv_post",
    )(y, bonus, g, ln_w.reshape(1, B_WIDTH), ln_b.reshape(1, B_WIDTH))


def _router_kernel(x_ref, g_ref, w_ref, b_ref, id_ref, wt_ref):
    x = x_ref[...]
    tm = x.shape[0]
    h = (x * lax.rsqrt(jnp.mean(x * x, axis=-1, keepdims=True) + RMS_EPS)) * g_ref[...]
    logits = _dot_f32(h, w_ref[...]) + b_ref[...]
    lane = lax.broadcasted_iota(jnp.int32, (tm, LANES), 1).astype(F32)
    none = float(LANES)
    is_grp = lane < N_GROUPS
    g_max = jnp.max(jnp.where(is_grp, logits, NEG_BIG), axis=-1, keepdims=True)
    grp = jnp.min(jnp.where(is_grp & (logits == g_max), lane, none), axis=-1, keepdims=True)
    p_grp = 1.0 / jnp.sum(jnp.where(is_grp, jnp.exp(logits - g_max), 0.0), axis=-1, keepdims=True)
    lo = N_GROUPS + EXPERTS_PER_GROUP * grp
    in_grp = (lane >= lo) & (lane < lo + EXPERTS_PER_GROUP)
    v1 = jnp.max(jnp.where(in_grp, logits, NEG_BIG), axis=-1, keepdims=True)
    i1 = jnp.min(jnp.where(in_grp & (logits == v1), lane, none), axis=-1, keepdims=True)
    rest = in_grp & (lane != i1)
    v2 = jnp.max(jnp.where(rest, logits, NEG_BIG), axis=-1, keepdims=True)
    i2 = jnp.min(jnp.where(rest & (logits == v2), lane, none), axis=-1, keepdims=True)
    e = jnp.exp(v2 - v1)
    w1 = p_grp / (1.0 + e)
    w2 = p_grp * e / (1.0 + e)
    ids = jnp.where(lane == 0.0, i1 - N_GROUPS, jnp.where(lane == 1.0, i2 - N_GROUPS, 0.0))
    id_ref[...] = ids.astype(jnp.int32)
    wt_ref[...] = jnp.where(lane == 0.0, w1, jnp.where(lane == 1.0, w2, 0.0))


def moe_router(x, gain, w_router, b_router, *, tm):
    m, k = x.shape
    return pl.pallas_call(
        _router_kernel,
        out_shape=(jax.ShapeDtypeStruct((m, LANES), jnp.int32), jax.ShapeDtypeStruct((m, LANES), F32)),
        grid=(m // tm,),
        in_specs=[pl.BlockSpec((tm, k), lambda i: (i, 0)), pl.BlockSpec((1, k), lambda i: (0, 0)),
                  pl.BlockSpec((k, LANES), lambda i: (0, 0)), pl.BlockSpec((1, LANES), lambda i: (0, 0))],
        out_specs=(pl.BlockSpec((tm, LANES), lambda i: (i, 0)), pl.BlockSpec((tm, LANES), lambda i: (i, 0))),
        compiler_params=_params(("parallel",)),
        name="moe_router",
    )(x, gain.reshape(1, k), w_router, b_router)


def _dispatch_kernel(src_ref, nrow_ref, g_ref, x_hbm, o_ref, buf, sem, *, tm):
    i = pl.program_id(0)
    live = i * tm < nrow_ref[0]

    def row_copy(k):
        return pltpu.make_async_copy(x_hbm.at[pl.ds(src_ref[i * tm + k], 1)], buf.at[pl.ds(k, 1)], sem.at[0])

    @pl.when(live)
    def _():
        def start(k, carry):
            row_copy(k).start()
            return carry

        def wait(k, carry):
            row_copy(k).wait()
            return carry

        lax.fori_loop(0, tm, start, 0, unroll=DMA_LOOP_UNROLL)
        lax.fori_loop(0, tm, wait, 0, unroll=DMA_LOOP_UNROLL)
        x = buf[...]
        h = (x * lax.rsqrt(jnp.mean(x * x, axis=-1, keepdims=True) + RMS_EPS)) * g_ref[...]
        o_ref[...] = h.astype(o_ref.dtype)

    @pl.when(jnp.logical_not(live))
    def _():
        o_ref[...] = jnp.zeros(o_ref.shape, o_ref.dtype)


def moe_dispatch(x_all, gain, src, n_rows_used, *, tm):
    rows = src.shape[0]
    k = x_all.shape[1]
    return pl.pallas_call(
        functools.partial(_dispatch_kernel, tm=tm),
        out_shape=jax.ShapeDtypeStruct((rows, k), BF16),
        grid_spec=pltpu.PrefetchScalarGridSpec(
            num_scalar_prefetch=2,
            grid=(rows // tm,),
            in_specs=[pl.BlockSpec((1, k), lambda i, s, n: (0, 0)), pl.BlockSpec(memory_space=pl.ANY)],
            out_specs=pl.BlockSpec((tm, k), lambda i, s, n: (i, 0)),
            scratch_shapes=[pltpu.VMEM((tm, k), F32), pltpu.SemaphoreType.DMA((1,))]),
        compiler_params=_params(("arbitrary",)),
        name="moe_dispatch",
    )(src, n_rows_used, gain.reshape(1, k), x_all)


def _expert_kernel(te_ref, nu_ref, x_ref, wg_ref, wu_ref, wd_ref, o_ref):
    i = pl.program_id(0)
    f = pl.program_id(1)
    used = i < nu_ref[0]

    @pl.when(used)
    def _():
        x = x_ref[...]
        hg = _dot(x, wg_ref[0, 0].astype(BF16))
        hu = _dot(x, wu_ref[0, 0].astype(BF16))
        act = (hg * _sigmoid(hg)) * hu
        part = _dot(act.astype(BF16), wd_ref[0, 0].astype(BF16))

        @pl.when(f == 0)
        def _():
            o_ref[...] = part

        @pl.when(f > 0)
        def _():
            o_ref[...] += part

    @pl.when(jnp.logical_not(used) & (f == 0))
    def _():
        o_ref[...] = jnp.zeros(o_ref.shape, F32)


def moe_experts(xs, tile_expert, n_used, w_gate, w_up, w_down, *, layer, tm, fc):
    rows, d = xs.shape
    n_tiles = rows // tm
    nf = EXPERT_HIDDEN // fc

    def live(i, f, te, nu):
        ok = i < nu[0]
        return jnp.where(ok, i, nu[0] - 1), jnp.where(ok, f, nf - 1)

    def x_map(i, f, te, nu):
        return live(i, f, te, nu)[0], 0

    def gu_map(i, f, te, nu):
        ii, ff = live(i, f, te, nu)
        return layer, te[ii], 0, ff

    def d_map(i, f, te, nu):
        ii, ff = live(i, f, te, nu)
        return layer, te[ii], ff, 0

    return pl.pallas_call(
        _expert_kernel,
        out_shape=jax.ShapeDtypeStruct((rows, d), F32),
        grid_spec=pltpu.PrefetchScalarGridSpec(
            num_scalar_prefetch=2,
            grid=(n_tiles, nf),
            in_specs=[pl.BlockSpec((tm, d), x_map),
                      pl.BlockSpec((1, 1, d, fc), gu_map),
                      pl.BlockSpec((1, 1, d, fc), gu_map),
                      pl.BlockSpec((1, 1, fc, d), d_map)],
            out_specs=pl.BlockSpec((tm, d), lambda i, f, te, nu: (i, 0))),
        compiler_params=_params(("arbitrary", "arbitrary"), vmem=56 * 1024 * 1024),
        name="moe_experts",
    )(tile_expert, n_used, xs, w_gate, w_up, w_down)


def _combine_kernel(dest_ref, x_ref, wt_ref, y_hbm, o_ref, buf, sem, *, tm, n_tok, tok0):
    i = pl.program_id(0)

    def row_copy(k, choice):
        src_row = dest_ref[choice * n_tok + tok0 + i * tm + k]
        return pltpu.make_async_copy(y_hbm.at[pl.ds(src_row, 1)], buf.at[choice, pl.ds(k, 1)], sem.at[choice])

    for choice in range(TOP_K):
        def start(k, carry, choice=choice):
            row_copy(k, choice).start()
            return carry

        lax.fori_loop(0, tm, start, 0, unroll=DMA_LOOP_UNROLL)
    for choice in range(TOP_K):
        def wait(k, carry, choice=choice):
            row_copy(k, choice).wait()
            return carry

        lax.fori_loop(0, tm, wait, 0, unroll=DMA_LOOP_UNROLL)
    wt = wt_ref[...]
    o_ref[...] = x_ref[...] + wt[:, 0:1] * buf[0] + wt[:, 1:2] * buf[1]


def moe_combine(x, y, dest_t, wts, *, tm, tok0):
    m, k = x.shape
    n_tok = dest_t.shape[0] // TOP_K
    row = pl.BlockSpec((tm, k), lambda i, dest: (i, 0))
    return pl.pallas_call(
        functools.partial(_combine_kernel, tm=tm, n_tok=n_tok, tok0=tok0),
        out_shape=jax.ShapeDtypeStruct((m, k), F32),
        grid_spec=pltpu.PrefetchScalarGridSpec(
            num_scalar_prefetch=1,
            grid=(m // tm,),
            in_specs=[row, pl.BlockSpec((tm, LANES), lambda i, dest: (i, 0)), pl.BlockSpec(memory_space=pl.ANY)],
            out_specs=row,
            scratch_shapes=[pltpu.VMEM((TOP_K, tm, k), F32), pltpu.SemaphoreType.DMA((TOP_K,))]),
        compiler_params=_params(("arbitrary",)),
        name="moe_combine",
    )(dest_t, x, wts, y)


def _routing_tables(ids, *, tm):
    n_assign = ids.shape[0] * TOP_K
    n_tiles = (n_assign + N_EXPERTS * (tm - 1)) // tm
    flat = ids.reshape(-1)
    onehot = (flat[:, None] == jnp.arange(N_EXPERTS, dtype=jnp.int32)[None, :]).astype(jnp.int32)
    counts = jnp.sum(onehot, axis=0)
    rank = jnp.take_along_axis(jnp.cumsum(onehot, axis=0), flat[:, None], axis=1)[:, 0] - 1
    tiles_per = (counts + tm - 1) // tm
    tile_end = jnp.cumsum(tiles_per)
    row_start = (tile_end - tiles_per) * tm
    dest = row_start[flat] + rank
    n_used = tile_end[-1]
    tile_ids = jnp.arange(n_tiles, dtype=jnp.int32)
    tile_expert = jnp.minimum(jnp.searchsorted(tile_end, tile_ids, side="right"), N_EXPERTS - 1).astype(jnp.int32)
    last_used = tile_expert[jnp.maximum(n_used - 1, 0)]
    tile_expert = jnp.where(tile_ids < n_used, tile_expert, last_used)
    src = jnp.zeros((n_tiles * tm,), jnp.int32).at[dest].set(jnp.arange(n_assign, dtype=jnp.int32) // TOP_K)
    return src, dest.reshape(-1, TOP_K), tile_expert, n_used.reshape(1).astype(jnp.int32)


def hier_moe_layer(xp, xs, layer, norm_ffn, router_w, router_b, w_gate, w_up, w_down, *, tm_e=512, fc=256):
    idp, wtp = moe_router(xp, norm_ffn[layer], router_w, router_b, tm=512)
    ids_, wts_ = moe_router(xs, norm_ffn[layer], router_w, router_b, tm=xs.shape[0])
    ids = jnp.concatenate([idp[:, :TOP_K], ids_[:, :TOP_K]], axis=0)
    src, dest, tile_expert, n_used = _routing_tables(ids, tm=tm_e)
    x_sorted = moe_dispatch(jnp.concatenate([xp, xs], axis=0), norm_ffn[layer], src, n_used * tm_e, tm=256)
    y = moe_experts(x_sorted, tile_expert, n_used, w_gate, w_up, w_down, layer=layer, tm=tm_e, fc=fc)
    dest_t = dest.T.reshape(-1)
    xp = moe_combine(xp, y, dest_t, wtp, tm=256, tok0=0)
    xs = moe_combine(xs, y, dest_t, wts_, tm=xs.shape[0], tok0=xp.shape[0])
    return xp, xs


def _pad_cols(w, n):
    return jnp.pad(w, ((0, 0), (0, n - w.shape[1])))


def kernel(x_prompt, x_sample, cache_a_k, cache_a_v, state_rwkv, state_shift, cache_sb_k, cache_sb_v, page_table,
           norm_mix, norm_ffn, norm_final, w_in0, lam_q1, lam_k1, lam_q2, lam_k2, subln0,
           rw_mu, rw_w0, rw_w2, rw_a0, rw_a2, rw_g2, rw_k_k, rw_k_a, rw_r_k, rw_ln_w, rw_ln_b, w_out0,
           w_qkv1, w_out1, router_grp_w, router_grp_b, router_exp_w, router_exp_b,
           exp_w_gate, exp_w_up, exp_w_down):
    batch, seq, d = x_prompt.shape
    s_n = x_sample.shape[0]
    n_p = batch * seq
    xp = x_prompt.reshape(n_p, d)
    xs = x_sample.reshape(s_n, d)
    n_phys = cache_a_k.shape[0]
    tm_p = 512

    lam_init = 0.8 - 0.6 * math.exp(-0.3 * 0)
    lam_rows = [z.reshape(1, A_HEAD_DIM) for z in (lam_q1, lam_k1, lam_q2, lam_k2)]
    w_q = w_in0[:, :A_WIDTH].astype(BF16)
    w_k = w_in0[:, A_WIDTH:2 * A_WIDTH].astype(BF16)
    w_v = w_in0[:, 2 * A_WIDTH:3 * A_WIDTH].astype(BF16)
    w_p = _pad_cols(w_in0[:, 3 * A_WIDTH:], B_PROJ_PAD).astype(BF16)
    g0 = norm_mix[0]

    (qp16,) = norm_matmul(xp, g0, w_q, (BF16,), tm=tm_p, tn=512)
    kp32, kp16 = norm_matmul(xp, g0, w_k, (F32, BF16), tm=tm_p, tn=512)
    vp32, vp16 = norm_matmul(xp, g0, w_v, (F32, BF16), tm=tm_p, tn=512)
    (pp,) = norm_matmul(xp, g0, w_p, (F32,), tm=tm_p, tn=1152)
    (qs32,) = norm_matmul(xs, g0, w_q, (F32,), tm=s_n, tn=512)
    (ks32,) = norm_matmul(xs, g0, w_k, (F32,), tm=s_n, tn=512)
    (vs32,) = norm_matmul(xs, g0, w_v, (F32,), tm=s_n, tn=512)
    (ps,) = norm_matmul(xs, g0, w_p, (F32,), tm=s_n, tn=1152)

    oa_p = diff_attn_prompt(qp16, kp16, vp16, lam_rows, subln0, batch=batch, seq=seq, lam_init=lam_init)
    oa_s = diff_attn_sample(qs32, ks32, vs32, cache_a_k.reshape(n_phys, PAGE_SIZE * A_HEADS, A_QK),
                            cache_a_v.reshape(n_phys, PAGE_SIZE * A_HEADS, A_QK), page_table, lam_rows, subln0,
                            lam_init=lam_init)

    zeros_tail = jnp.zeros((B_TAIL - B_DECAY_LORA - B_A_LORA - B_G_LORA, B_WIDTH), F32)
    w2_pad = jnp.concatenate([rw_w2, jnp.zeros((B_A_LORA + B_G_LORA, B_WIDTH), F32), zeros_tail], axis=0)
    a2_pad = jnp.concatenate([jnp.zeros((B_DECAY_LORA, B_WIDTH), F32), rw_a2,
                              jnp.zeros((B_G_LORA, B_WIDTH), F32), zeros_tail], axis=0)
    g2_pad = jnp.concatenate([jnp.zeros((B_DECAY_LORA + B_A_LORA, B_WIDTH), F32), rw_g2, zeros_tail], axis=0)
    row = lambda z: z.reshape(1, -1)
    prep_w = (row(jnp.pad(rw_mu, (0, B_PROJ_PAD - B_PROJ))), row(rw_w0), row(rw_a0), row(rw_k_k), row(rw_k_a),
              row(rw_r_k), w2_pad, a2_pad, g2_pad)

    _, d_n, k_n, na_n, b_n, v_n, g_n, bonus_n, yw_n, vkr_n = rwkv_prep(
        pp, jnp.zeros((batch, 1, B_PROJ_PAD), F32), prep_w, batch=batch, seq=seq, tm=256, sequential=True)
    y_p, rwkv_p = rwkv_scan(na_n, d_n, b_n, k_n, v_n, yw_n, vkr_n)
    ob_p = rwkv_post(y_p.reshape(n_p, B_WIDTH), bonus_n.reshape(n_p, B_WIDTH), g_n.reshape(n_p, B_WIDTH),
                     rw_ln_w, rw_ln_b, tm=tm_p)

    prev_s = jnp.pad(state_shift.reshape(s_n, B_PROJ), ((0, 0), (0, B_PROJ_PAD - B_PROJ)))
    r_s, d_s, k_s, na_s, b_s, v_s, g_s, bonus_s = (z.reshape(s_n, B_WIDTH) for z in rwkv_prep(
        ps, prev_s, prep_w, batch=s_n, seq=1, tm=s_n, sequential=False)[:8])
    y_s, rwkv_s = rwkv_step(na_s, r_s, d_s, b_s, k_s, v_s, state_rwkv)
    ob_s = rwkv_post(y_s, bonus_s, g_s, rw_ln_w, rw_ln_b, tm=s_n)

    w_o0 = w_out0.astype(BF16)
    xp = matmul_residual([oa_p, ob_p], w_o0, xp, tm=tm_p, tn=512)
    xs = matmul_residual([oa_s, ob_s], w_o0, xs, tm=s_n, tn=512)

    shift_p = pp.reshape(batch, seq, B_PROJ_PAD)[:, seq - 1:, :B_PROJ]
    shift_s = ps[:, :B_PROJ].reshape(s_n, 1, B_PROJ)
    a_k_p = kp32.reshape(batch, seq, A_HEADS, A_QK)
    a_v_p = vp32.reshape(batch, seq, A_HEADS, A_QK)
    a_k_s = ks32.reshape(s_n, 1, A_HEADS, A_QK)
    a_v_s = vs32.reshape(s_n, 1, A_HEADS, A_QK)

    def router_weights(layer):
        w = jnp.concatenate([router_grp_w[layer], router_exp_w[layer]], axis=1)
        b = jnp.concatenate([router_grp_b[layer], router_exp_b[layer]], axis=0)
        return _pad_cols(w, LANES), jnp.pad(b, (0, LANES - b.shape[0])).reshape(1, LANES)

    rw0, rb0 = router_weights(0)
    xp, xs = hier_moe_layer(xp, xs, 0, norm_ffn, rw0, rb0, exp_w_gate, exp_w_up, exp_w_down)

    g1 = norm_mix[1]
    w_q1 = w_qkv1[:, :d].astype(BF16)
    w_k1 = w_qkv1[:, d:2 * d].astype(BF16)
    w_v1 = w_qkv1[:, 2 * d:].astype(BF16)
    (qp16,) = norm_matmul(xp, g1, w_q1, (BF16,), tm=tm_p, tn=512)
    kp32, kp16 = norm_matmul(xp, g1, w_k1, (F32, BF16), tm=tm_p, tn=512)
    vp32, vp16 = norm_matmul(xp, g1, w_v1, (F32, BF16), tm=tm_p, tn=512)
    (qs32,) = norm_matmul(xs, g1, w_q1, (F32,), tm=s_n, tn=512)
    (ks32,) = norm_matmul(xs, g1, w_k1, (F32,), tm=s_n, tn=512)
    (vs32,) = norm_matmul(xs, g1, w_v1, (F32,), tm=s_n, tn=512)

    oc_p = sb_attn_prompt(qp16, kp16, vp16, batch=batch, seq=seq)
    oc_s = sb_attn_sample(qs32, cache_sb_k.reshape(n_phys, PAGE_SIZE * C_HEADS, C_HEAD_DIM),
                          cache_sb_v.reshape(n_phys, PAGE_SIZE * C_HEADS, C_HEAD_DIM), page_table)
    w_o1 = w_out1.astype(BF16)
    xp = matmul_residual([oc_p], w_o1, xp, tm=tm_p, tn=512)
    xs = matmul_residual([oc_s], w_o1, xs, tm=s_n, tn=512)

    sb_k_p = kp32.reshape(batch, seq, C_HEADS, C_HEAD_DIM)
    sb_v_p = vp32.reshape(batch, seq, C_HEADS, C_HEAD_DIM)
    sb_k_s = ks32.reshape(s_n, 1, C_HEADS, C_HEAD_DIM)
    sb_v_s = vs32.reshape(s_n, 1, C_HEADS, C_HEAD_DIM)

    rw1, rb1 = router_weights(1)
    xp, xs = hier_moe_layer(xp, xs, 1, norm_ffn, rw1, rb1, exp_w_gate, exp_w_up, exp_w_down)

    y_prompt = rmsnorm_rows(xp, norm_final, tm=tm_p).reshape(batch, seq, d)
    y_sample = rmsnorm_rows(xs, norm_final, tm=s_n).reshape(s_n, 1, d)
    return (y_prompt, y_sample, a_k_p, a_v_p, a_k_s, a_v_s, rwkv_p, rwkv_s, shift_p, shift_s,
            sb_k_p, sb_v_p, sb_k_s, sb_v_s)
```

```python
import functools
import math

import jax
import jax.numpy as jnp
from jax import lax
from jax.experimental import pallas as pl
from jax.experimental.pallas import tpu as pltpu

F32 = jnp.float32
BF16 = jnp.bfloat16
HIGHEST = lax.Precision.HIGHEST

D_MODEL = 2048
A_WIDTH = 1024
A_HEAD_DIM = 64
A_HEADS = 8
A_QK = 128
B_WIDTH = 1024
B_HEAD = 64
B_HEADS = 16
B_DECAY_LORA = 64
B_A_LORA = 64
B_G_LORA = 160
B_PROJ = 3 * B_WIDTH + B_DECAY_LORA + B_A_LORA + B_G_LORA
B_PROJ_PAD = 3456
B_TAIL = B_PROJ_PAD - 3 * B_WIDTH
C_HEADS = 16
C_HEAD_DIM = 128
N_GROUPS = 4
EXPERTS_PER_GROUP = 8
N_EXPERTS = 32
TOP_K = 2
EXPERT_HIDDEN = 1024
PAGE_SIZE = 128
RMS_EPS = 1e-6
GN_EPS = 64e-5
NEG_BIG = -1e30

LANES = 128
SUBLANES = 8
V7X_VMEM_BYTES = 64 * 1024 * 1024
VMEM_LIMIT = 48 * 1024 * 1024
DMA_LOOP_UNROLL = 8
PAGE_BUFFERS = 4


def _params(sem, vmem=VMEM_LIMIT):
    return pltpu.CompilerParams(dimension_semantics=sem, vmem_limit_bytes=vmem)


def _dot(a, b):
    return jnp.dot(a, b, preferred_element_type=F32)


def _dot_nt(a, b):
    return lax.dot_general(a, b, (((1,), (1,)), ((), ())), preferred_element_type=F32)


def _dot_f32(a, b):
    return jnp.dot(a, b, preferred_element_type=F32, precision=HIGHEST)


def _norm_matmul_kernel(x_ref, g_ref, w_ref, *refs, n_out):
    outs, h_scr = refs[:n_out], refs[n_out]

    @pl.when(pl.program_id(1) == 0)
    def _():
        x = x_ref[...]
        y = x * lax.rsqrt(jnp.mean(x * x, axis=-1, keepdims=True) + RMS_EPS)
        h_scr[...] = (y * g_ref[...]).astype(BF16)

    acc = _dot(h_scr[...], w_ref[...])
    for o in outs:
        o[...] = acc.astype(o.dtype)


def norm_matmul(x, gain, w, out_dtypes, *, tm, tn):
    m, k = x.shape
    n = w.shape[1]
    assert m % tm == 0 and n % tn == 0
    outs = tuple(jax.ShapeDtypeStruct((m, n), dt) for dt in out_dtypes)
    return pl.pallas_call(
        functools.partial(_norm_matmul_kernel, n_out=len(outs)),
        out_shape=outs,
        grid=(m // tm, n // tn),
        in_specs=[pl.BlockSpec((tm, k), lambda i, j: (i, 0)),
                  pl.BlockSpec((1, k), lambda i, j: (0, 0)),
                  pl.BlockSpec((k, tn), lambda i, j: (0, j))],
        out_specs=tuple(pl.BlockSpec((tm, tn), lambda i, j: (i, j)) for _ in outs),
        scratch_shapes=[pltpu.VMEM((tm, k), BF16)],
        compiler_params=_params(("parallel", "arbitrary")),
        name="norm_matmul",
    )(x, gain.reshape(1, k), w)


def _matmul_res_kernel(*refs, n_a):
    a_refs, w_refs, r_ref, o_ref = refs[:n_a], refs[n_a:2 * n_a], refs[2 * n_a], refs[2 * n_a + 1]
    acc = r_ref[...]
    for a, w in zip(a_refs, w_refs):
        acc = acc + _dot(a[...], w[...])
    o_ref[...] = acc


def matmul_residual(a_list, w, resid, *, tm, tn):
    m, n = resid.shape
    n_a = len(a_list)
    kk = a_list[0].shape[1]
    assert all(a.shape == (m, kk) for a in a_list) and w.shape == (n_a * kk, n)
    in_specs = [pl.BlockSpec((tm, kk), lambda i, j: (i, 0)) for _ in a_list]
    in_specs += [pl.BlockSpec((kk, tn), lambda i, j, c=c: (c, j)) for c in range(n_a)]
    in_specs += [pl.BlockSpec((tm, tn), lambda i, j: (i, j))]
    return pl.pallas_call(
        functools.partial(_matmul_res_kernel, n_a=n_a),
        out_shape=jax.ShapeDtypeStruct((m, n), F32),
        grid=(m // tm, n // tn),
        in_specs=in_specs,
        out_specs=pl.BlockSpec((tm, tn), lambda i, j: (i, j)),
        compiler_params=_params(("parallel", "parallel")),
        name="matmul_residual",
    )(*a_list, *([w] * n_a), resid)


def _rmsnorm_kernel(x_ref, g_ref, o_ref):
    x = x_ref[...]
    y = x * lax.rsqrt(jnp.mean(x * x, axis=-1, keepdims=True) + RMS_EPS)
    o_ref[...] = y * g_ref[...]


def rmsnorm_rows(x, gain, *, tm):
    m, k = x.shape
    return pl.pallas_call(
        _rmsnorm_kernel,
        out_shape=jax.ShapeDtypeStruct((m, k), F32),
        grid=(m // tm,),
        in_specs=[pl.BlockSpec((tm, k), lambda i: (i, 0)), pl.BlockSpec((1, k), lambda i: (0, 0))],
        out_specs=pl.BlockSpec((tm, k), lambda i: (i, 0)),
        compiler_params=_params(("parallel",)),
        name="final_rmsnorm",
    )(x, gain.reshape(1, k))


def _lam_value(lq1, lk1, lq2, lk2, lam_init):
    s1 = jnp.sum(lq1[...] * lk1[...], axis=-1, keepdims=True)
    s2 = jnp.sum(lq2[...] * lk2[...], axis=-1, keepdims=True)
    return jnp.exp(s1) - jnp.exp(s2) + lam_init


def _subln(o, sub_ref, lam_init):
    y = o * lax.rsqrt(jnp.mean(o * o, axis=-1, keepdims=True) + RMS_EPS)
    return (y * sub_ref[...]) * (1.0 - lam_init)


def _softmax_step(s, m, l, acc, v):
    m_new = jnp.maximum(m, jnp.max(s, axis=-1, keepdims=True))
    alpha = jnp.exp(m - m_new)
    p = jnp.exp(s - m_new)
    l = alpha * l + jnp.sum(p, axis=-1, keepdims=True)
    acc = alpha * acc + _dot(p.astype(BF16), v)
    return m_new, l, acc


def _diff_attn_prompt_kernel(q_ref, k_ref, v_ref, lq1, lk1, lq2, lk2, sub_ref, o_ref, *, tq, lam_init):
    h = pl.program_id(1)
    i = pl.program_id(2)
    scale = A_HEAD_DIM ** -0.5
    q = q_ref[...]
    lane = lax.broadcasted_iota(jnp.int32, (tq, A_QK), 1)
    q1 = jnp.where(lane < A_HEAD_DIM, q, jnp.zeros_like(q))
    q2 = jnp.where(lane >= A_HEAD_DIM, q, jnp.zeros_like(q))
    hh = (h + 1).astype(F32) * (8.0 / A_HEADS)
    slope = jnp.exp2(jnp.zeros((1, tq), F32) - hh)
    qpos = (i * tq).astype(F32) + lax.broadcasted_iota(jnp.int32, (tq, 1), 0).astype(F32)
    kiota = lax.broadcasted_iota(jnp.int32, (1, tq), 1).astype(F32)

    def tile(j, carry, masked):
        m1, l1, a1, m2, l2, a2 = carry
        start = pl.multiple_of(j * tq, tq)
        kj = k_ref[pl.ds(start, tq), :]
        vj = v_ref[pl.ds(start, tq), :]
        dist = qpos - ((j * tq).astype(F32) + kiota)
        bias = slope * dist
        s1 = _dot_nt(q1, kj) * scale - bias
        s2 = _dot_nt(q2, kj) * scale - bias
        if masked:
            ok = dist >= 0.0
            s1 = jnp.where(ok, s1, NEG_BIG)
            s2 = jnp.where(ok, s2, NEG_BIG)
        m1, l1, a1 = _softmax_step(s1, m1, l1, a1, vj)
        m2, l2, a2 = _softmax_step(s2, m2, l2, a2, vj)
        return m1, l1, a1, m2, l2, a2

    z1 = jnp.zeros((tq, 1), F32)
    za = jnp.zeros((tq, A_QK), F32)
    init = (z1 + NEG_BIG, z1, za, z1 + NEG_BIG, z1, za)
    carry = lax.fori_loop(0, i, lambda j, c: tile(j, c, False), init)
    m1, l1, a1, m2, l2, a2 = tile(i, carry, True)
    lam = _lam_value(lq1, lk1, lq2, lk2, lam_init)
    o = a1 / l1 - lam * (a2 / l2)
    o_ref[...] = _subln(o, sub_ref, lam_init).astype(o_ref.dtype)


def diff_attn_prompt(q, k, v, lam_rows, subln, *, batch, seq, lam_init, tq=512):
    nq = seq // tq
    small = [pl.BlockSpec((1, A_HEAD_DIM), lambda b, h, i: (0, 0)) for _ in range(4)]
    return pl.pallas_call(
        functools.partial(_diff_attn_prompt_kernel, tq=tq, lam_init=lam_init),
        out_shape=jax.ShapeDtypeStruct((batch * seq, A_WIDTH), BF16),
        grid=(batch, A_HEADS, nq),
        in_specs=[pl.BlockSpec((tq, A_QK), lambda b, h, i: (b * nq + i, h)),
                  pl.BlockSpec((seq, A_QK), lambda b, h, i: (b, h)),
                  pl.BlockSpec((seq, A_QK), lambda b, h, i: (b, h))] + small
                 + [pl.BlockSpec((1, A_QK), lambda b, h, i: (0, 0))],
        out_specs=pl.BlockSpec((tq, A_QK), lambda b, h, i: (b * nq + i, h)),
        compiler_params=_params(("parallel", "parallel", "arbitrary")),
        name="diff_attn_prompt",
    )(q, k, v, *lam_rows, subln.reshape(1, A_QK))


def _page_pipeline(pt_ref, k_hbm, v_hbm, kbuf, vbuf, sem, *, n_steps, pps, page_of):
    t = pl.program_id(0) * n_steps + pl.program_id(1)
    total = pl.num_programs(0) * n_steps
    depth = PAGE_BUFFERS - 1

    def copies(tt):
        seq, step, slot = tt // n_steps, tt % n_steps, tt % PAGE_BUFFERS
        out = []
        for n in range(pps):
            page = pt_ref[seq, page_of(step, n)]
            out.append(pltpu.make_async_copy(k_hbm.at[page], kbuf.at[slot, n], sem.at[slot, 0, n]))
            out.append(pltpu.make_async_copy(v_hbm.at[page], vbuf.at[slot, n], sem.at[slot, 1, n]))
        return out

    @pl.when(t == 0)
    def _():
        for d in range(depth):
            for c in copies(d):
                c.start()

    @pl.when(t + depth < total)
    def _():
        for c in copies(t + depth):
            c.start()

    for c in copies(t):
        c.wait()
    return t % PAGE_BUFFERS


def _page_pipeline_scratch(pps, n_keys, width):
    return [pltpu.VMEM((PAGE_BUFFERS, pps, n_keys, width), F32), pltpu.VMEM((PAGE_BUFFERS, pps, n_keys, width), F32),
            pltpu.SemaphoreType.DMA((PAGE_BUFFERS, 2, pps))]


def _diff_attn_sample_kernel(pt_ref, q_ref, kn_ref, vn_ref, k_hbm, v_hbm, lq1, lk1, lq2, lk2, sub_ref,
                             o_ref, m_scr, l_scr, acc_scr, kbuf, vbuf, sem, *, n_pages, pps, lam_init):
    slot = _page_pipeline(pt_ref, k_hbm, v_hbm, kbuf, vbuf, sem, n_steps=n_pages // pps, pps=pps,
                          page_of=lambda step, n: step * pps + n)
    p = pl.program_id(1)
    rows = 2 * A_HEADS
    n_keys = PAGE_SIZE * A_HEADS * pps
    scale = A_HEAD_DIM ** -0.5

    @pl.when(p == 0)
    def _():
        m_scr[...] = jnp.full((rows, 1), NEG_BIG, F32)
        l_scr[...] = jnp.zeros((rows, 1), F32)
        acc_scr[...] = jnp.zeros((rows, A_QK), F32)

    q = q_ref[0]
    lane = lax.broadcasted_iota(jnp.int32, (A_HEADS, A_QK), 1)
    qm = jnp.concatenate([jnp.where(lane < A_HEAD_DIM, q, 0.0), jnp.where(lane >= A_HEAD_DIM, q, 0.0)], axis=0)
    row = lax.broadcasted_iota(jnp.int32, (rows, n_keys), 0)
    col = lax.broadcasted_iota(jnp.int32, (rows, n_keys), 1)
    own = (col % A_HEADS) == (row % A_HEADS)
    head1 = (lax.broadcasted_iota(jnp.int32, (rows, 1), 0) % A_HEADS + 1).astype(F32) * (8.0 / A_HEADS)
    slope = jnp.exp2(-head1)
    kpos = (p * (PAGE_SIZE * pps)).astype(F32) + (col // A_HEADS).astype(F32)
    dist = float(n_pages * PAGE_SIZE) - kpos
    qb = qm.astype(BF16)
    s = jnp.concatenate([_dot_nt(qb, kbuf[slot, n].astype(BF16)) for n in range(pps)], axis=1)
    s = jnp.where(own, s * scale - slope * dist, NEG_BIG)
    m0, l0 = m_scr[...], l_scr[...]
    m = jnp.maximum(m0, jnp.max(s, axis=-1, keepdims=True))
    alpha = jnp.exp(m0 - m)
    pr = jnp.exp(s - m)
    l = alpha * l0 + jnp.sum(pr, axis=-1, keepdims=True)
    acc = alpha * acc_scr[...]
    page_keys = PAGE_SIZE * A_HEADS
    for n in range(pps):
        acc = acc + _dot(pr[:, n * page_keys:(n + 1) * page_keys].astype(BF16), vbuf[slot, n].astype(BF16))
    m_scr[...] = m
    l_scr[...] = l
    acc_scr[...] = acc

    @pl.when(p == n_pages // pps - 1)
    def _():
        kn2 = jnp.concatenate([kn_ref[0], kn_ref[0]], axis=0)
        vn2 = jnp.concatenate([vn_ref[0], vn_ref[0]], axis=0)
        s_new = jnp.sum(qm * kn2, axis=-1, keepdims=True) * scale
        m_new = jnp.maximum(m, s_new)
        alpha = jnp.exp(m - m_new)
        p_new = jnp.exp(s_new - m_new)
        o16 = (alpha * acc + p_new * vn2) / (alpha * l + p_new)
        lam = _lam_value(lq1, lk1, lq2, lk2, lam_init)
        o = o16[:A_HEADS] - lam * o16[A_HEADS:]
        o_ref[0] = _subln(o, sub_ref, lam_init).astype(o_ref.dtype)


def diff_attn_sample(q, k_new, v_new, cache_k, cache_v, page_table, lam_rows, subln, *, lam_init):
    s_n, n_pages = page_table.shape
    pps = 4
    assert n_pages % pps == 0
    n_keys = PAGE_SIZE * A_HEADS
    small = [pl.BlockSpec((1, A_HEAD_DIM), lambda s, p, pt: (0, 0)) for _ in range(4)]
    row_spec = pl.BlockSpec((1, A_HEADS, A_QK), lambda s, p, pt: (s, 0, 0))
    hbm = pl.BlockSpec(memory_space=pl.ANY)
    rows = 2 * A_HEADS
    heads = lambda z: z.reshape(s_n, A_HEADS, A_QK)
    out = pl.pallas_call(
        functools.partial(_diff_attn_sample_kernel, n_pages=n_pages, pps=pps, lam_init=lam_init),
        out_shape=jax.ShapeDtypeStruct((s_n, A_HEADS, A_QK), BF16),
        grid_spec=pltpu.PrefetchScalarGridSpec(
            num_scalar_prefetch=1,
            grid=(s_n, n_pages // pps),
            in_specs=[row_spec, row_spec, row_spec, hbm, hbm] + small
                     + [pl.BlockSpec((1, A_QK), lambda s, p, pt: (0, 0))],
            out_specs=pl.BlockSpec((1, A_HEADS, A_QK), lambda s, p, pt: (s, 0, 0)),
            scratch_shapes=[pltpu.VMEM((rows, 1), F32), pltpu.VMEM((rows, 1), F32),
                            pltpu.VMEM((rows, A_QK), F32)] + _page_pipeline_scratch(pps, n_keys, A_QK)),
        compiler_params=_params(("arbitrary", "arbitrary")),
        name="diff_attn_sample",
    )(page_table, heads(q), heads(k_new), heads(v_new), cache_k, cache_v, *lam_rows, subln.reshape(1, A_QK))
    return out.reshape(s_n, A_WIDTH)


def _log_sigmoid(z):
    return jnp.minimum(z, 0.0) - jnp.log1p(jnp.exp(-jnp.abs(z)))


def _suffix_sum(lk, tri):
    hi = lk.astype(BF16)
    lo = (lk - hi.astype(F32)).astype(BF16)
    return _dot(hi, tri) + _dot(lo, tri)


def _strict_lower_ones(n):
    r = lax.broadcasted_iota(jnp.int32, (n, n), 0)
    c = lax.broadcasted_iota(jnp.int32, (n, n), 1)
    return jnp.where(r > c, 1.0, 0.0).astype(BF16)


def _sb_prompt_kernel(q_ref, k_ref, v_ref, o_ref, *, tq, hpg):
    i = pl.program_id(2)
    scale = C_HEAD_DIM ** -0.5
    tri = _strict_lower_ones(tq)
    rr = lax.broadcasted_iota(jnp.int32, (tq, tq), 0)
    cc = lax.broadcasted_iota(jnp.int32, (tq, tq), 1)
    before = cc < rr
    qs = [q_ref[:, h * C_HEAD_DIM:(h + 1) * C_HEAD_DIM] for h in range(hpg)]

    def tile(j, carry, masked):
        start = pl.multiple_of(j * tq, tq)
        out = []
        for h in range(hpg):
            c, acc = carry[h]
            lanes = slice(h * C_HEAD_DIM, (h + 1) * C_HEAD_DIM)
            kj = k_ref[pl.ds(start, tq), lanes]
            vj = v_ref[pl.ds(start, tq), lanes]
            z = _dot_nt(qs[h], kj) * scale
            ls = _log_sigmoid(z)
            lk = ls - z
            if masked:
                lk = jnp.where(before, lk, 0.0)
            later = c + _suffix_sum(lk, tri)
            att = jnp.exp(ls + later)
            if masked:
                att = jnp.where(before, att, 0.0)
            acc = acc + _dot(att.astype(BF16), vj)
            c = c + jnp.sum(lk, axis=-1, keepdims=True)
            out.append((c, acc))
        return tuple(out)

    zero = (jnp.zeros((tq, 1), F32), jnp.zeros((tq, C_HEAD_DIM), F32))
    carry = tile(i, (zero,) * hpg, True)
    carry = lax.fori_loop(0, i, lambda jj, c: tile(i - 1 - jj, c, False), carry)
    o_ref[...] = jnp.concatenate([acc for _, acc in carry], axis=1).astype(o_ref.dtype)


def sb_attn_prompt(q, k, v, *, batch, seq, tq=512, hpg=2):
    nq = seq // tq
    width = C_HEADS * C_HEAD_DIM
    wl = hpg * C_HEAD_DIM
    return pl.pallas_call(
        functools.partial(_sb_prompt_kernel, tq=tq, hpg=hpg),
        out_shape=jax.ShapeDtypeStruct((batch * seq, width), BF16),
        grid=(batch, C_HEADS // hpg, nq),
        in_specs=[pl.BlockSpec((tq, wl), lambda b, h, i: (b * nq + i, h)),
                  pl.BlockSpec((seq, wl), lambda b, h, i: (b, h)),
                  pl.BlockSpec((seq, wl), lambda b, h, i: (b, h))],
        out_specs=pl.BlockSpec((tq, wl), lambda b, h, i: (b * nq + i, h)),
        compiler_params=_params(("parallel", "parallel", "arbitrary")),
        name="sb_attn_prompt",
    )(q, k, v)


def _sb_sample_kernel(pt_ref, q_ref, k_hbm, v_hbm, o_ref, c_scr, acc_scr, kbuf, vbuf, sem, *, n_pages):
    p = pl.program_id(1)
    n_tiles = PAGE_SIZE * C_HEADS // LANES
    scale = C_HEAD_DIM ** -0.5

    @pl.when(p == 0)
    def _():
        c_scr[...] = jnp.zeros((1, LANES), F32)
        acc_scr[...] = jnp.zeros((C_HEADS, C_HEAD_DIM), F32)

    slot = _page_pipeline(pt_ref, k_hbm, v_hbm, kbuf, vbuf, sem, n_steps=n_pages, pps=1,
                          page_of=lambda step, n: n_pages - 1 - step)
    kf = kbuf[slot, 0].astype(BF16)
    vf = vbuf[slot, 0].astype(BF16)
    zt = _dot_nt(q_ref[0].astype(BF16), kf)
    row = lax.broadcasted_iota(jnp.int32, (C_HEADS, LANES), 0)
    lane = lax.broadcasted_iota(jnp.int32, (C_HEADS, LANES), 1)
    own = (lane % C_HEADS) == row
    z = jnp.concatenate([jnp.sum(jnp.where(own, zt[:, g * LANES:(g + 1) * LANES], 0.0), axis=0, keepdims=True)
                         for g in range(n_tiles)], axis=0) * scale
    ls = _log_sigmoid(z)
    lk = ls - z
    li = lax.broadcasted_iota(jnp.int32, (LANES, LANES), 0)
    lj = lax.broadcasted_iota(jnp.int32, (LANES, LANES), 1)
    same = (li % C_HEADS) == (lj % C_HEADS)
    hi = lk.astype(BF16)
    lo = (lk - hi.astype(F32)).astype(BF16)
    sel = jnp.concatenate([jnp.where(same & (li > lj), 1.0, 0.0), jnp.where(same, 1.0, 0.0)], axis=1).astype(BF16)
    both = _dot(hi, sel) + _dot(lo, sel)
    inside, total = both[:, :LANES], both[:, LANES:]
    run = c_scr[...]
    later = [None] * n_tiles
    for g in range(n_tiles - 1, -1, -1):
        later[g] = run + inside[g:g + 1]
        run = run + total[g:g + 1]
    c_scr[...] = run
    att = jnp.exp(ls + jnp.concatenate(later, axis=0))
    spread = jnp.concatenate([jnp.where(own, att[g:g + 1], 0.0) for g in range(n_tiles)], axis=1)
    acc = acc_scr[...] + _dot(spread.astype(BF16), vf)
    acc_scr[...] = acc

    @pl.when(p == pl.num_programs(1) - 1)
    def _():
        o_ref[0] = acc.astype(o_ref.dtype)


def sb_attn_sample(q, cache_k, cache_v, page_table):
    s_n, n_pages = page_table.shape
    n_keys = PAGE_SIZE * C_HEADS
    hbm = pl.BlockSpec(memory_space=pl.ANY)
    out = pl.pallas_call(
        functools.partial(_sb_sample_kernel, n_pages=n_pages),
        out_shape=jax.ShapeDtypeStruct((s_n, C_HEADS, C_HEAD_DIM), BF16),
        grid_spec=pltpu.PrefetchScalarGridSpec(
            num_scalar_prefetch=1,
            grid=(s_n, n_pages),
            in_specs=[pl.BlockSpec((1, C_HEADS, C_HEAD_DIM), lambda s, p, pt: (s, 0, 0)), hbm, hbm],
            out_specs=pl.BlockSpec((1, C_HEADS, C_HEAD_DIM), lambda s, p, pt: (s, 0, 0)),
            scratch_shapes=[pltpu.VMEM((1, LANES), F32), pltpu.VMEM((C_HEADS, C_HEAD_DIM), F32)]
                           + _page_pipeline_scratch(1, n_keys, C_HEAD_DIM)),
        compiler_params=_params(("arbitrary", "arbitrary")),
        name="sb_attn_sample",
    )(page_table, q.reshape(s_n, C_HEADS, C_HEAD_DIM), cache_k, cache_v)
    return out.reshape(s_n, C_HEADS * C_HEAD_DIM)


def _head_sum(x):
    r = lax.broadcasted_iota(jnp.int32, (LANES, LANES), 0) // B_HEAD
    c = lax.broadcasted_iota(jnp.int32, (LANES, LANES), 1) // B_HEAD
    ones = jnp.where(r == c, 1.0, 0.0).astype(F32)
    parts = [_dot_f32(x[:, g * LANES:(g + 1) * LANES], ones) for g in range(x.shape[1] // LANES)]
    return jnp.concatenate(parts, axis=1)


def _softplus(x):
    return jnp.maximum(x, 0.0) + jnp.log1p(jnp.exp(-jnp.abs(x)))


def _sigmoid(x):
    return 1.0 / (1.0 + jnp.exp(-x))


def _rwkv_prep_kernel(p_ref, prev_ref, mu_ref, w0_ref, a0_ref, kk_ref, ka_ref, rk_ref, w2_ref, a2_ref, g2_ref,
                      *refs, sequential):
    outs = refs[:10]
    p = p_ref[...]
    tm = p.shape[0]
    if sequential:
        last_scr = refs[10]
        t = pl.program_id(1)

        @pl.when(t == 0)
        def _():
            last_scr[...] = prev_ref[0]

        rolled = pltpu.roll(p, 1, axis=0)
        row = lax.broadcasted_iota(jnp.int32, (tm, 1), 0)
        shifted = jnp.where(row == 0, last_scr[...], rolled)
        last_scr[...] = p[tm - 1:tm, :]
    else:
        shifted = prev_ref[...]
    xs = p + (shifted - p) * mu_ref[...]
    r = xs[:, :B_WIDTH]
    k = xs[:, B_WIDTH:2 * B_WIDTH]
    v = xs[:, 2 * B_WIDTH:3 * B_WIDTH]
    tail = xs[:, 3 * B_WIDTH:]
    low_rank = lambda z, w_ref: _dot(z.astype(BF16), w_ref[...].astype(BF16))
    w = -_softplus(-(w0_ref[...] + low_rank(jnp.tanh(tail), w2_ref))) - 0.5
    decay = jnp.exp(-jnp.exp(w))
    a = _sigmoid(a0_ref[...] + low_rank(tail, a2_ref))
    g = low_rank(_sigmoid(tail), g2_ref)
    kk = k * kk_ref[...]
    kk = kk * lax.rsqrt(jnp.maximum(_head_sum(kk * kk), 1e-24))
    k_h = k * (1.0 + (a - 1.0) * ka_ref[...])
    bonus = _head_sum(r * k_h * rk_ref[...]) * v
    b = kk * a
    r_o, d_o, k_o, na_o, b_o, v_o, g_o, bonus_o, yw_o, vkr_o = outs
    r_o[0] = r
    d_o[0] = decay
    k_o[0] = k_h
    na_o[0] = -kk
    b_o[0] = b
    v_o[0] = v
    g_o[0] = g
    bonus_o[0] = bonus
    yw_o[0] = decay * r - kk * _head_sum(b * r)
    vkr_o[0] = v * _head_sum(k_h * r)


def rwkv_prep(p, prev, weights, *, batch, seq, tm, sequential):
    nt = seq // tm if sequential else 1
    nb = batch if sequential else (batch * seq) // tm
    vec = lambda n: pl.BlockSpec((1, n), lambda b, t: (0, 0))
    mat = lambda: pl.BlockSpec((B_TAIL, B_WIDTH), lambda b, t: (0, 0))
    if sequential:
        p_spec = pl.BlockSpec((tm, B_PROJ_PAD), lambda b, t: (b * nt + t, 0))
        prev_spec = pl.BlockSpec((1, 1, B_PROJ_PAD), lambda b, t: (b, 0, 0))
        n_shape = jax.ShapeDtypeStruct((batch, seq, B_WIDTH), F32)
        n_spec = pl.BlockSpec((1, tm, B_WIDTH), lambda b, t: (b, t, 0))
        scratch = [pltpu.VMEM((1, B_PROJ_PAD), F32)]
    else:
        p_spec = pl.BlockSpec((tm, B_PROJ_PAD), lambda b, t: (b, 0))
        prev_spec = pl.BlockSpec((tm, B_PROJ_PAD), lambda b, t: (b, 0))
        n_shape = jax.ShapeDtypeStruct((nb, tm, B_WIDTH), F32)
        n_spec = pl.BlockSpec((1, tm, B_WIDTH), lambda b, t: (b, 0, 0))
        scratch = []
    return pl.pallas_call(
        functools.partial(_rwkv_prep_kernel, sequential=sequential),
        out_shape=(n_shape,) * 10,
        grid=(nb, nt),
        in_specs=[p_spec, prev_spec, vec(B_PROJ_PAD), vec(B_WIDTH), vec(B_WIDTH), vec(B_WIDTH), vec(B_WIDTH),
                  vec(B_WIDTH), mat(), mat(), mat()],
        out_specs=(n_spec,) * 10,
        scratch_shapes=scratch,
        compiler_params=_params(("parallel", "arbitrary")),
        name="rwkv_prep",
    )(p, prev, *weights)


def _rwkv_scan_kernel(a_ref, d_ref, b_ref, k_ref, v_ref, yw_ref, vkr_ref, y_ref, s_ref, st_scr, *, nb, tc, n_groups):
    t_idx = pl.program_id(1)
    n_pairs = B_HEADS // 2

    @pl.when(t_idx == 0)
    def _():
        st_scr[...] = jnp.zeros(st_scr.shape, F32)

    r_i = lax.broadcasted_iota(jnp.int32, (B_HEAD, LANES), 0)
    l_i = lax.broadcasted_iota(jnp.int32, (B_HEAD, LANES), 1)
    diag = (l_i % B_HEAD) == r_i
    kr = lax.broadcasted_iota(jnp.int32, (LANES, LANES), 0) // B_HEAD
    kc = lax.broadcasted_iota(jnp.int32, (LANES, LANES), 1) // B_HEAD
    ones_blk = jnp.where(kr == kc, 1.0, 0.0).astype(BF16)
    chains = [(bb, pr) for bb in range(nb) for pr in range(n_pairs)]
    per = len(chains) // n_groups
    groups = [chains[g * per:(g + 1) * per] for g in range(n_groups)]
    refs = dict(a=a_ref, d=d_ref, b=b_ref, k=k_ref, v=v_ref, yw=yw_ref, vkr=vkr_ref)

    def block8(blk, carry):
        t0 = pl.multiple_of(blk * SUBLANES, SUBLANES)
        rows8 = {n: [ref[bb, pl.ds(t0, SUBLANES), :] for bb in range(nb)] for n, ref in refs.items()}
        y_rows = [[[None] * n_pairs for _ in range(SUBLANES)] for _ in range(nb)]
        for cc in range(SUBLANES):
            for grp in groups:
                row = lambda n, bb, pr: rows8[n][bb][cc:cc + 1, pr * LANES:(pr + 1) * LANES]
                lhs = []
                for bb, pr in grp:
                    st = st_scr[bb * n_pairs + pr]
                    lhs.append((st * row("a", bb, pr)).astype(BF16))
                    lhs.append(jnp.where(diag, row("v", bb, pr), 0.0).astype(BF16))
                    lhs.append((st * row("yw", bb, pr) + jnp.where(diag, row("vkr", bb, pr), 0.0)).astype(BF16))
                res = _dot(jnp.concatenate(lhs, axis=0), ones_blk)
                for n, (bb, pr) in enumerate(grp):
                    base = 3 * n * B_HEAD
                    sa_b = res[base:base + B_HEAD]
                    v_b = res[base + B_HEAD:base + 2 * B_HEAD]
                    y_b = res[base + 2 * B_HEAD:base + 3 * B_HEAD]
                    st = st_scr[bb * n_pairs + pr]
                    st_scr[bb * n_pairs + pr] = st * row("d", bb, pr) + sa_b * row("b", bb, pr) + v_b * row("k", bb, pr)
                    y_rows[bb][cc][pr] = jnp.sum(jnp.where(diag, y_b, 0.0), axis=0, keepdims=True)
        for bb in range(nb):
            y8 = [jnp.concatenate(y_rows[bb][cc], axis=1) for cc in range(SUBLANES)]
            y_ref[bb, pl.ds(t0, SUBLANES), :] = jnp.concatenate(y8, axis=0)
        return carry

    lax.fori_loop(0, tc // SUBLANES, block8, 0)

    @pl.when(t_idx == pl.num_programs(1) - 1)
    def _():
        for bb, pr in chains:
            st = st_scr[bb * n_pairs + pr]
            for half in range(2):
                s_ref[bb, 2 * pr + half] = st[:, half * B_HEAD:(half + 1) * B_HEAD]


def rwkv_scan(na, d, b, k, v, yw, vkr, *, nb=2, tc=128, n_groups=1):
    batch, seq, _ = v.shape
    spec = pl.BlockSpec((nb, tc, B_WIDTH), lambda g, t: (g, t, 0))
    return pl.pallas_call(
        functools.partial(_rwkv_scan_kernel, nb=nb, tc=tc, n_groups=n_groups),
        out_shape=(jax.ShapeDtypeStruct((batch, seq, B_WIDTH), F32),
                   jax.ShapeDtypeStruct((batch, B_HEADS, B_HEAD, B_HEAD), F32)),
        grid=(batch // nb, seq // tc),
        in_specs=[spec] * 7,
        out_specs=(spec, pl.BlockSpec((nb, B_HEADS, B_HEAD, B_HEAD), lambda g, t: (g, 0, 0, 0))),
        scratch_shapes=[pltpu.VMEM((nb * B_HEADS // 2, B_HEAD, LANES), F32)],
        compiler_params=_params(("parallel", "arbitrary")),
        name="rwkv_scan",
    )(na, d, b, k, v, yw, vkr)


def _rwkv_step_kernel(a_ref, r_ref, d_ref, b_ref, k_ref, v_ref, s0_ref, y_ref, s_ref):
    r_i = lax.broadcasted_iota(jnp.int32, (B_HEAD, B_HEAD), 0)
    c_i = lax.broadcasted_iota(jnp.int32, (B_HEAD, B_HEAD), 1)
    eye = jnp.where(r_i == c_i, 1.0, 0.0).astype(F32)
    for h in range(B_HEADS):
        row = lambda ref: ref[0, h:h + 1, :]
        st = s0_ref[0, h]
        sa = jnp.sum(st * row(a_ref), axis=1, keepdims=True)
        v_col = jnp.sum(eye * row(v_ref), axis=1, keepdims=True)
        st = st * row(d_ref) + sa * row(b_ref) + v_col * row(k_ref)
        s_ref[0, h] = st
        y_col = jnp.sum(st * row(r_ref), axis=1, keepdims=True)
        y_ref[0, h:h + 1, :] = jnp.sum(eye * y_col, axis=0, keepdims=True)


def rwkv_step(na, r, d, b, k, v, s0):
    s_n = s0.shape[0]
    hs = lambda z: z.reshape(s_n, B_HEADS, B_HEAD)
    vec = pl.BlockSpec((1, B_HEADS, B_HEAD), lambda s: (s, 0, 0))
    st = pl.BlockSpec((1, B_HEADS, B_HEAD, B_HEAD), lambda s: (s, 0, 0, 0))
    y, s_new = pl.pallas_call(
        _rwkv_step_kernel,
        out_shape=(jax.ShapeDtypeStruct((s_n, B_HEADS, B_HEAD), F32),
                   jax.ShapeDtypeStruct(s0.shape, F32)),
        grid=(s_n,),
        in_specs=[vec] * 6 + [st],
        out_specs=(vec, st),
        compiler_params=_params(("parallel",)),
        name="rwkv_step",
    )(hs(na), hs(r), hs(d), hs(b), hs(k), hs(v), s0)
    return y.reshape(s_n, B_WIDTH), s_new


def _rwkv_post_kernel(y_ref, bonus_ref, g_ref, lw_ref, lb_ref, o_ref):
    y = y_ref[...]
    mean = _head_sum(y) * (1.0 / B_HEAD)
    yc = y - mean
    var = _head_sum(yc * yc) * (1.0 / B_HEAD)
    out = yc * lax.rsqrt(var + GN_EPS) * lw_ref[...] + lb_ref[...]
    o_ref[...] = ((out + bonus_ref[...]) * g_ref[...]).astype(o_ref.dtype)


def rwkv_post(y, bonus, g, ln_w, ln_b, *, tm):
    m = y.shape[0]
    row = pl.BlockSpec((tm, B_WIDTH), lambda i: (i, 0))
    vec = pl.BlockSpec((1, B_WIDTH), lambda i: (0, 0))
    return pl.pallas_call(
        _rwkv_post_kernel,
        out_shape=jax.ShapeDtypeStruct((m, B_WIDTH), BF16),
        grid=(m // tm,),
        in_specs=[row, row, row, vec, vec],
        out_specs=row,
        compiler_params=_params(("parallel",)),
        name="rwkv_post",
    )(y, bonus, g, ln_w.reshape(1, B_WIDTH), ln_b.reshape(1, B_WIDTH))


def _router_kernel(x_ref, g_ref, w_ref, b_ref, id_ref, wt_ref):
    x = x_ref[...]
    tm = x.shape[0]
    h = (x * lax.rsqrt(jnp.mean(x * x, axis=-1, keepdims=True) + RMS_EPS)) * g_ref[...]
    logits = _dot(h.astype(BF16), w_ref[...].astype(BF16)) + b_ref[...]
    lane = lax.broadcasted_iota(jnp.int32, (tm, LANES), 1).astype(F32)
    none = float(LANES)
    is_grp = lane < N_GROUPS
    g_max = jnp.max(jnp.where(is_grp, logits, NEG_BIG), axis=-1, keepdims=True)
    grp = jnp.min(jnp.where(is_grp & (logits == g_max), lane, none), axis=-1, keepdims=True)
    p_grp = 1.0 / jnp.sum(jnp.where(is_grp, jnp.exp(logits - g_max), 0.0), axis=-1, keepdims=True)
    lo = N_GROUPS + EXPERTS_PER_GROUP * grp
    in_grp = (lane >= lo) & (lane < lo + EXPERTS_PER_GROUP)
    v1 = jnp.max(jnp.where(in_grp, logits, NEG_BIG), axis=-1, keepdims=True)
    i1 = jnp.min(jnp.where(in_grp & (logits == v1), lane, none), axis=-1, keepdims=True)
    rest = in_grp & (lane != i1)
    v2 = jnp.max(jnp.where(rest, logits, NEG_BIG), axis=-1, keepdims=True)
    i2 = jnp.min(jnp.where(rest & (logits == v2), lane, none), axis=-1, keepdims=True)
    e = jnp.exp(v2 - v1)
    w1 = p_grp / (1.0 + e)
    w2 = p_grp * e / (1.0 + e)
    ids = jnp.where(lane == 0.0, i1 - N_GROUPS, jnp.where(lane == 1.0, i2 - N_GROUPS, 0.0))
    id_ref[...] = ids.astype(jnp.int32)
    wt_ref[...] = jnp.where(lane == 0.0, w1, jnp.where(lane == 1.0, w2, 0.0))


def moe_router(x, gain, w_router, b_router, *, tm):
    m, k = x.shape
    return pl.pallas_call(
        _router_kernel,
        out_shape=(jax.ShapeDtypeStruct((m, LANES), jnp.int32), jax.ShapeDtypeStruct((m, LANES), F32)),
        grid=(m // tm,),
        in_specs=[pl.BlockSpec((tm, k), lambda i: (i, 0)), pl.BlockSpec((1, k), lambda i: (0, 0)),
                  pl.BlockSpec((k, LANES), lambda i: (0, 0)), pl.BlockSpec((1, LANES), lambda i: (0, 0))],
        out_specs=(pl.BlockSpec((tm, LANES), lambda i: (i, 0)), pl.BlockSpec((tm, LANES), lambda i: (i, 0))),
        compiler_params=_params(("parallel",)),
        name="moe_router",
    )(x, gain.reshape(1, k), w_router, b_router)


def _dispatch_kernel(src_ref, nrow_ref, g_ref, x_hbm, o_ref, buf, sem, *, tm):
    i = pl.program_id(0)
    live = i * tm < nrow_ref[0]

    def row_copy(tile, k):
        slot = tile % 2
        return pltpu.make_async_copy(x_hbm.at[pl.ds(src_ref[tile * tm + k], 1)], buf.at[slot, pl.ds(k, 1)],
                                     sem.at[slot])

    def start_tile(tile):
        def start(k, carry):
            row_copy(tile, k).start()
            return carry

        lax.fori_loop(0, tm, start, 0, unroll=DMA_LOOP_UNROLL)

    @pl.when(i == 0)
    def _():
        start_tile(0)

    @pl.when((i + 1) * tm < nrow_ref[0])
    def _():
        start_tile(i + 1)

    @pl.when(live)
    def _():
        def wait(k, carry):
            row_copy(i, k).wait()
            return carry

        lax.fori_loop(0, tm, wait, 0, unroll=DMA_LOOP_UNROLL)
        x = buf[i % 2]
        h = (x * lax.rsqrt(jnp.mean(x * x, axis=-1, keepdims=True) + RMS_EPS)) * g_ref[...]
        o_ref[...] = h.astype(o_ref.dtype)

    @pl.when(jnp.logical_not(live))
    def _():
        o_ref[...] = jnp.zeros(o_ref.shape, o_ref.dtype)


def moe_dispatch(x_all, gain, src, n_rows_used, *, tm):
    rows = src.shape[0]
    k = x_all.shape[1]
    return pl.pallas_call(
        functools.partial(_dispatch_kernel, tm=tm),
        out_shape=jax.ShapeDtypeStruct((rows, k), BF16),
        grid_spec=pltpu.PrefetchScalarGridSpec(
            num_scalar_prefetch=2,
            grid=(rows // tm,),
            in_specs=[pl.BlockSpec((1, k), lambda i, s, n: (0, 0)), pl.BlockSpec(memory_space=pl.ANY)],
            out_specs=pl.BlockSpec((tm, k), lambda i, s, n: (i, 0)),
            scratch_shapes=[pltpu.VMEM((2, tm, k), F32), pltpu.SemaphoreType.DMA((2,))]),
        compiler_params=_params(("arbitrary",)),
        name="moe_dispatch",
    )(src, n_rows_used, gain.reshape(1, k), x_all)


def _expert_kernel(te_ref, nu_ref, x_ref, wg_ref, wu_ref, wd_ref, o_ref):
    i = pl.program_id(0)
    f = pl.program_id(1)
    used = i < nu_ref[0]

    @pl.when(used)
    def _():
        x = x_ref[...]
        hg = _dot(x, wg_ref[0, 0].astype(BF16))
        hu = _dot(x, wu_ref[0, 0].astype(BF16))
        act = (hg * _sigmoid(hg)) * hu
        part = _dot(act.astype(BF16), wd_ref[0, 0].astype(BF16))

        @pl.when(f == 0)
        def _():
            o_ref[...] = part

        @pl.when(f > 0)
        def _():
            o_ref[...] += part

    @pl.when(jnp.logical_not(used) & (f == 0))
    def _():
        o_ref[...] = jnp.zeros(o_ref.shape, F32)


def moe_experts(xs, tile_expert, n_used, w_gate, w_up, w_down, *, layer, tm, fc):
    rows, d = xs.shape
    n_tiles = rows // tm
    nf = EXPERT_HIDDEN // fc

    def live(i, f, te, nu):
        ok = i < nu[0]
        return jnp.where(ok, i, nu[0] - 1), jnp.where(ok, f, nf - 1)

    def x_map(i, f, te, nu):
        return live(i, f, te, nu)[0], 0

    def gu_map(i, f, te, nu):
        ii, ff = live(i, f, te, nu)
        return layer, te[ii], 0, ff

    def d_map(i, f, te, nu):
        ii, ff = live(i, f, te, nu)
        return layer, te[ii], ff, 0

    return pl.pallas_call(
        _expert_kernel,
        out_shape=jax.ShapeDtypeStruct((rows, d), F32),
        grid_spec=pltpu.PrefetchScalarGridSpec(
            num_scalar_prefetch=2,
            grid=(n_tiles, nf),
            in_specs=[pl.BlockSpec((tm, d), x_map),
                      pl.BlockSpec((1, 1, d, fc), gu_map),
                      pl.BlockSpec((1, 1, d, fc), gu_map),
                      pl.BlockSpec((1, 1, fc, d), d_map)],
            out_specs=pl.BlockSpec((tm, d), lambda i, f, te, nu: (i, 0))),
        compiler_params=_params(("arbitrary", "arbitrary"), vmem=56 * 1024 * 1024),
        name="moe_experts",
    )(tile_expert, n_used, xs, w_gate, w_up, w_down)


def _combine_kernel(dest_ref, x_ref, wt_ref, y_hbm, o_ref, buf, sem, *, tm, n_tok, tok0):
    i = pl.program_id(0)

    def row_copy(k, choice):
        src_row = dest_ref[choice * n_tok + tok0 + i * tm + k]
        return pltpu.make_async_copy(y_hbm.at[pl.ds(src_row, 1)], buf.at[choice, pl.ds(k, 1)], sem.at[choice])

    for choice in range(TOP_K):
        def start(k, carry, choice=choice):
            row_copy(k, choice).start()
            return carry

        lax.fori_loop(0, tm, start, 0, unroll=DMA_LOOP_UNROLL)
    for choice in range(TOP_K):
        def wait(k, carry, choice=choice):
            row_copy(k, choice).wait()
            return carry

        lax.fori_loop(0, tm, wait, 0, unroll=DMA_LOOP_UNROLL)
    wt = wt_ref[...]
    o_ref[...] = x_ref[...] + wt[:, 0:1] * buf[0] + wt[:, 1:2] * buf[1]


def moe_combine(x, y, dest_t, wts, *, tm, tok0):
    m, k = x.shape
    n_tok = dest_t.shape[0] // TOP_K
    row = pl.BlockSpec((tm, k), lambda i, dest: (i, 0))
    return pl.pallas_call(
        functools.partial(_combine_kernel, tm=tm, n_tok=n_tok, tok0=tok0),
        out_shape=jax.ShapeDtypeStruct((m, k), F32),
        grid_spec=pltpu.PrefetchScalarGridSpec(
            num_scalar_prefetch=1,
            grid=(m // tm,),
            in_specs=[row, pl.BlockSpec((tm, LANES), lambda i, dest: (i, 0)), pl.BlockSpec(memory_space=pl.ANY)],
            out_specs=row,
            scratch_shapes=[pltpu.VMEM((TOP_K, tm, k), F32), pltpu.SemaphoreType.DMA((TOP_K,))]),
        compiler_params=_params(("arbitrary",)),
        name="moe_combine",
    )(dest_t, x, wts, y)


def _routing_tables(ids, *, tm):
    n_assign = ids.shape[0] * TOP_K
    n_tiles = (n_assign + N_EXPERTS * (tm - 1)) // tm
    flat = ids.reshape(-1)
    onehot = (flat[:, None] == jnp.arange(N_EXPERTS, dtype=jnp.int32)[None, :]).astype(jnp.int32)
    counts = jnp.sum(onehot, axis=0)
    rank = jnp.take_along_axis(jnp.cumsum(onehot, axis=0), flat[:, None], axis=1)[:, 0] - 1
    tiles_per = (counts + tm - 1) // tm
    tile_end = jnp.cumsum(tiles_per)
    row_start = (tile_end - tiles_per) * tm
    dest = row_start[flat] + rank
    n_used = tile_end[-1]
    tile_ids = jnp.arange(n_tiles, dtype=jnp.int32)
    tile_expert = jnp.minimum(jnp.searchsorted(tile_end, tile_ids, side="right"), N_EXPERTS - 1).astype(jnp.int32)
    last_used = tile_expert[jnp.maximum(n_used - 1, 0)]
    tile_expert = jnp.where(tile_ids < n_used, tile_expert, last_used)
    src = jnp.zeros((n_tiles * tm,), jnp.int32).at[dest].set(jnp.arange(n_assign, dtype=jnp.int32) // TOP_K)
    return src, dest.reshape(-1, TOP_K), tile_expert, n_used.reshape(1).astype(jnp.int32)


def hier_moe_layer(xp, xs, layer, norm_ffn, router_w, router_b, w_gate, w_up, w_down, *, tm_e=512, fc=512):
    idp, wtp = moe_router(xp, norm_ffn[layer], router_w, router_b, tm=512)
    ids_, wts_ = moe_router(xs, norm_ffn[layer], router_w, router_b, tm=xs.shape[0])
    ids = jnp.concatenate([idp[:, :TOP_K], ids_[:, :TOP_K]], axis=0)
    src, dest, tile_expert, n_used = _routing_tables(ids, tm=tm_e)
    x_sorted = moe_dispatch(jnp.concatenate([xp, xs], axis=0), norm_ffn[layer], src, n_used * tm_e, tm=256)
    y = moe_experts(x_sorted, tile_expert, n_used, w_gate, w_up, w_down, layer=layer, tm=tm_e, fc=fc)
    dest_t = dest.T.reshape(-1)
    xp = moe_combine(xp, y, dest_t, wtp, tm=256, tok0=0)
    xs = moe_combine(xs, y, dest_t, wts_, tm=xs.shape[0], tok0=xp.shape[0])
    return xp, xs


def _pad_cols(w, n):
    return jnp.pad(w, ((0, 0), (0, n - w.shape[1])))


def kernel(x_prompt, x_sample, cache_a_k, cache_a_v, state_rwkv, state_shift, cache_sb_k, cache_sb_v, page_table,
           norm_mix, norm_ffn, norm_final, w_in0, lam_q1, lam_k1, lam_q2, lam_k2, subln0,
           rw_mu, rw_w0, rw_w2, rw_a0, rw_a2, rw_g2, rw_k_k, rw_k_a, rw_r_k, rw_ln_w, rw_ln_b, w_out0,
           w_qkv1, w_out1, router_grp_w, router_grp_b, router_exp_w, router_exp_b,
           exp_w_gate, exp_w_up, exp_w_down):
    batch, seq, d = x_prompt.shape
    s_n = x_sample.shape[0]
    n_p = batch * seq
    xp = x_prompt.reshape(n_p, d)
    xs = x_sample.reshape(s_n, d)
    n_phys = cache_a_k.shape[0]
    tm_p = 512

    lam_init = 0.8 - 0.6 * math.exp(-0.3 * 0)
    lam_rows = [z.reshape(1, A_HEAD_DIM) for z in (lam_q1, lam_k1, lam_q2, lam_k2)]
    w_q = w_in0[:, :A_WIDTH].astype(BF16)
    w_k = w_in0[:, A_WIDTH:2 * A_WIDTH].astype(BF16)
    w_v = w_in0[:, 2 * A_WIDTH:3 * A_WIDTH].astype(BF16)
    w_p = _pad_cols(w_in0[:, 3 * A_WIDTH:], B_PROJ_PAD).astype(BF16)
    g0 = norm_mix[0]

    (qp16,) = norm_matmul(xp, g0, w_q, (BF16,), tm=tm_p, tn=512)
    kp32, kp16 = norm_matmul(xp, g0, w_k, (F32, BF16), tm=tm_p, tn=512)
    vp32, vp16 = norm_matmul(xp, g0, w_v, (F32, BF16), tm=tm_p, tn=512)
    (pp,) = norm_matmul(xp, g0, w_p, (F32,), tm=tm_p, tn=1152)
    (qs32,) = norm_matmul(xs, g0, w_q, (F32,), tm=s_n, tn=512)
    (ks32,) = norm_matmul(xs, g0, w_k, (F32,), tm=s_n, tn=512)
    (vs32,) = norm_matmul(xs, g0, w_v, (F32,), tm=s_n, tn=512)
    (ps,) = norm_matmul(xs, g0, w_p, (F32,), tm=s_n, tn=1152)

    oa_p = diff_attn_prompt(qp16, kp16, vp16, lam_rows, subln0, batch=batch, seq=seq, lam_init=lam_init)
    oa_s = diff_attn_sample(qs32, ks32, vs32, cache_a_k.reshape(n_phys, PAGE_SIZE * A_HEADS, A_QK),
                            cache_a_v.reshape(n_phys, PAGE_SIZE * A_HEADS, A_QK), page_table, lam_rows, subln0,
                            lam_init=lam_init)

    zeros_tail = jnp.zeros((B_TAIL - B_DECAY_LORA - B_A_LORA - B_G_LORA, B_WIDTH), F32)
    w2_pad = jnp.concatenate([rw_w2, jnp.zeros((B_A_LORA + B_G_LORA, B_WIDTH), F32), zeros_tail], axis=0)
    a2_pad = jnp.concatenate([jnp.zeros((B_DECAY_LORA, B_WIDTH), F32), rw_a2,
                              jnp.zeros((B_G_LORA, B_WIDTH), F32), zeros_tail], axis=0)
    g2_pad = jnp.concatenate([jnp.zeros((B_DECAY_LORA + B_A_LORA, B_WIDTH), F32), rw_g2, zeros_tail], axis=0)
    row = lambda z: z.reshape(1, -1)
    prep_w = (row(jnp.pad(rw_mu, (0, B_PROJ_PAD - B_PROJ))), row(rw_w0), row(rw_a0), row(rw_k_k), row(rw_k_a),
              row(rw_r_k), w2_pad, a2_pad, g2_pad)

    _, d_n, k_n, na_n, b_n, v_n, g_n, bonus_n, yw_n, vkr_n = rwkv_prep(
        pp, jnp.zeros((batch, 1, B_PROJ_PAD), F32), prep_w, batch=batch, seq=seq, tm=256, sequential=True)
    y_p, rwkv_p = rwkv_scan(na_n, d_n, b_n, k_n, v_n, yw_n, vkr_n)
    ob_p = rwkv_post(y_p.reshape(n_p, B_WIDTH), bonus_n.reshape(n_p, B_WIDTH), g_n.reshape(n_p, B_WIDTH),
                     rw_ln_w, rw_ln_b, tm=tm_p)

    prev_s = jnp.pad(state_shift.reshape(s_n, B_PROJ), ((0, 0), (0, B_PROJ_PAD - B_PROJ)))
    r_s, d_s, k_s, na_s, b_s, v_s, g_s, bonus_s = (z.reshape(s_n, B_WIDTH) for z in rwkv_prep(
        ps, prev_s, prep_w, batch=s_n, seq=1, tm=s_n, sequential=False)[:8])
    y_s, rwkv_s = rwkv_step(na_s, r_s, d_s, b_s, k_s, v_s, state_rwkv)
    ob_s = rwkv_post(y_s, bonus_s, g_s, rw_ln_w, rw_ln_b, tm=s_n)

    w_o0 = w_out0.astype(BF16)
    xp = matmul_residual([oa_p, ob_p], w_o0, xp, tm=tm_p, tn=512)
    xs = matmul_residual([oa_s, ob_s], w_o0, xs, tm=s_n, tn=512)

    shift_p = pp.reshape(batch, seq, B_PROJ_PAD)[:, seq - 1:, :B_PROJ]
    shift_s = ps[:, :B_PROJ].reshape(s_n, 1, B_PROJ)
    a_k_p = kp32.reshape(batch, seq, A_HEADS, A_QK)
    a_v_p = vp32.reshape(batch, seq, A_HEADS, A_QK)
    a_k_s = ks32.reshape(s_n, 1, A_HEADS, A_QK)
    a_v_s = vs32.reshape(s_n, 1, A_HEADS, A_QK)

    def router_weights(layer):
        w = jnp.concatenate([router_grp_w[layer], router_exp_w[layer]], axis=1)
        b = jnp.concatenate([router_grp_b[layer], router_exp_b[layer]], axis=0)
        return _pad_cols(w, LANES), jnp.pad(b, (0, LANES - b.shape[0])).reshape(1, LANES)

    rw0, rb0 = router_weights(0)
    xp, xs = hier_moe_layer(xp, xs, 0, norm_ffn, rw0, rb0, exp_w_gate, exp_w_up, exp_w_down)

    g1 = norm_mix[1]
    w_q1 = w_qkv1[:, :d].astype(BF16)
    w_k1 = w_qkv1[:, d:2 * d].astype(BF16)
    w_v1 = w_qkv1[:, 2 * d:].astype(BF16)
    (qp16,) = norm_matmul(xp, g1, w_q1, (BF16,), tm=tm_p, tn=512)
    kp32, kp16 = norm_matmul(xp, g1, w_k1, (F32, BF16), tm=tm_p, tn=512)
    vp32, vp16 = norm_matmul(xp, g1, w_v1, (F32, BF16), tm=tm_p, tn=512)
    (qs32,) = norm_matmul(xs, g1, w_q1, (F32,), tm=s_n, tn=512)
    (ks32,) = norm_matmul(xs, g1, w_k1, (F32,), tm=s_n, tn=512)
    (vs32,) = norm_matmul(xs, g1, w_v1, (F32,), tm=s_n, tn=512)

    oc_p = sb_attn_prompt(qp16, kp16, vp16, batch=batch, seq=seq)
    oc_s = sb_attn_sample(qs32, cache_sb_k.reshape(n_phys, PAGE_SIZE * C_HEADS, C_HEAD_DIM),
                          cache_sb_v.reshape(n_phys, PAGE_SIZE * C_HEADS, C_HEAD_DIM), page_table)
    w_o1 = w_out1.astype(BF16)
    xp = matmul_residual([oc_p], w_o1, xp, tm=tm_p, tn=512)
    xs = matmul_residual([oc_s], w_o1, xs, tm=s_n, tn=512)

    sb_k_p = kp32.reshape(batch, seq, C_HEADS, C_HEAD_DIM)
    sb_v_p = vp32.reshape(batch, seq, C_HEADS, C_HEAD_DIM)
    sb_k_s = ks32.reshape(s_n, 1, C_HEADS, C_HEAD_DIM)
    sb_v_s = vs32.reshape(s_n, 1, C_HEADS, C_HEAD_DIM)

    rw1, rb1 = router_weights(1)
    xp, xs = hier_moe_layer(xp, xs, 1, norm_ffn, rw1, rb1, exp_w_gate, exp_w_up, exp_w_down)

    y_prompt = rmsnorm_rows(xp, norm_final, tm=tm_p).reshape(batch, seq, d)
    y_sample = rmsnorm_rows(xs, norm_final, tm=s_n).reshape(s_n, 1, d)
    return (y_prompt, y_sample, a_k_p, a_v_p, a_k_s, a_v_s, rwkv_p, rwkv_s, shift_p, shift_s,
            sb_k_p, sb_v_p, sb_k_s, sb_v_s)
```

```python
import functools
import math

import jax
import jax.numpy as jnp
from jax import lax
from jax.experimental import pallas as pl
from jax.experimental.pallas import tpu as pltpu

F32 = jnp.float32
BF16 = jnp.bfloat16
HIGHEST = lax.Precision.HIGHEST

D_MODEL = 2048
A_WIDTH = 1024
A_HEAD_DIM = 64
A_HEADS = 8
A_QK = 128
B_WIDTH = 1024
B_HEAD = 64
B_HEADS = 16
B_DECAY_LORA = 64
B_A_LORA = 64
B_G_LORA = 160
B_PROJ = 3 * B_WIDTH + B_DECAY_LORA + B_A_LORA + B_G_LORA
B_PROJ_PAD = 3456
B_TAIL = B_PROJ_PAD - 3 * B_WIDTH
C_HEADS = 16
C_HEAD_DIM = 128
N_GROUPS = 4
EXPERTS_PER_GROUP = 8
N_EXPERTS = 32
TOP_K = 2
EXPERT_HIDDEN = 1024
PAGE_SIZE = 128
RMS_EPS = 1e-6
GN_EPS = 64e-5
NEG_BIG = -1e30

LANES = 128
SUBLANES = 8
V7X_VMEM_BYTES = 64 * 1024 * 1024
VMEM_LIMIT = 48 * 1024 * 1024
DMA_LOOP_UNROLL = 8
PAGE_BUFFERS = 4


def _params(sem, vmem=VMEM_LIMIT):
    return pltpu.CompilerParams(dimension_semantics=sem, vmem_limit_bytes=vmem)


def _dot(a, b):
    return jnp.dot(a, b, preferred_element_type=F32)


def _dot_nt(a, b):
    return lax.dot_general(a, b, (((1,), (1,)), ((), ())), preferred_element_type=F32)


def _dot_f32(a, b):
    return jnp.dot(a, b, preferred_element_type=F32, precision=HIGHEST)


def _norm_matmul_kernel(x_ref, g_ref, w_ref, *refs, n_out):
    outs, h_scr = refs[:n_out], refs[n_out]

    @pl.when(pl.program_id(1) == 0)
    def _():
        x = x_ref[...]
        y = x * lax.rsqrt(jnp.mean(x * x, axis=-1, keepdims=True) + RMS_EPS)
        h_scr[...] = (y * g_ref[...]).astype(BF16)

    acc = _dot(h_scr[...], w_ref[...])
    for o in outs:
        o[...] = acc.astype(o.dtype)


def norm_matmul(x, gain, w, out_dtypes, *, tm, tn):
    m, k = x.shape
    n = w.shape[1]
    assert m % tm == 0 and n % tn == 0
    outs = tuple(jax.ShapeDtypeStruct((m, n), dt) for dt in out_dtypes)
    return pl.pallas_call(
        functools.partial(_norm_matmul_kernel, n_out=len(outs)),
        out_shape=outs,
        grid=(m // tm, n // tn),
        in_specs=[pl.BlockSpec((tm, k), lambda i, j: (i, 0)),
                  pl.BlockSpec((1, k), lambda i, j: (0, 0)),
                  pl.BlockSpec((k, tn), lambda i, j: (0, j))],
        out_specs=tuple(pl.BlockSpec((tm, tn), lambda i, j: (i, j)) for _ in outs),
        scratch_shapes=[pltpu.VMEM((tm, k), BF16)],
        compiler_params=_params(("parallel", "arbitrary")),
        name="norm_matmul",
    )(x, gain.reshape(1, k), w)


def _matmul_res_kernel(*refs, n_a):
    a_refs, w_refs, r_ref, o_ref = refs[:n_a], refs[n_a:2 * n_a], refs[2 * n_a], refs[2 * n_a + 1]
    acc = r_ref[...]
    for a, w in zip(a_refs, w_refs):
        acc = acc + _dot(a[...], w[...])
    o_ref[...] = acc


def matmul_residual(a_list, w, resid, *, tm, tn):
    m, n = resid.shape
    n_a = len(a_list)
    kk = a_list[0].shape[1]
    assert all(a.shape == (m, kk) for a in a_list) and w.shape == (n_a * kk, n)
    in_specs = [pl.BlockSpec((tm, kk), lambda i, j: (i, 0)) for _ in a_list]
    in_specs += [pl.BlockSpec((kk, tn), lambda i, j, c=c: (c, j)) for c in range(n_a)]
    in_specs += [pl.BlockSpec((tm, tn), lambda i, j: (i, j))]
    return pl.pallas_call(
        functools.partial(_matmul_res_kernel, n_a=n_a),
        out_shape=jax.ShapeDtypeStruct((m, n), F32),
        grid=(m // tm, n // tn),
        in_specs=in_specs,
        out_specs=pl.BlockSpec((tm, tn), lambda i, j: (i, j)),
        compiler_params=_params(("parallel", "parallel")),
        name="matmul_residual",
    )(*a_list, *([w] * n_a), resid)


def _rmsnorm_kernel(x_ref, g_ref, o_ref):
    x = x_ref[...]
    y = x * lax.rsqrt(jnp.mean(x * x, axis=-1, keepdims=True) + RMS_EPS)
    o_ref[...] = y * g_ref[...]


def rmsnorm_rows(x, gain, *, tm):
    m, k = x.shape
    return pl.pallas_call(
        _rmsnorm_kernel,
        out_shape=jax.ShapeDtypeStruct((m, k), F32),
        grid=(m // tm,),
        in_specs=[pl.BlockSpec((tm, k), lambda i: (i, 0)), pl.BlockSpec((1, k), lambda i: (0, 0))],
        out_specs=pl.BlockSpec((tm, k), lambda i: (i, 0)),
        compiler_params=_params(("parallel",)),
        name="final_rmsnorm",
    )(x, gain.reshape(1, k))


def _lam_value(lq1, lk1, lq2, lk2, lam_init):
    s1 = jnp.sum(lq1[...] * lk1[...], axis=-1, keepdims=True)
    s2 = jnp.sum(lq2[...] * lk2[...], axis=-1, keepdims=True)
    return jnp.exp(s1) - jnp.exp(s2) + lam_init


def _subln(o, sub_ref, lam_init):
    y = o * lax.rsqrt(jnp.mean(o * o, axis=-1, keepdims=True) + RMS_EPS)
    return (y * sub_ref[...]) * (1.0 - lam_init)


def _softmax_step(s, m, l, acc, v):
    m_new = jnp.maximum(m, jnp.max(s, axis=-1, keepdims=True))
    alpha = jnp.exp(m - m_new)
    p = jnp.exp(s - m_new)
    l = alpha * l + jnp.sum(p, axis=-1, keepdims=True)
    acc = alpha * acc + _dot(p.astype(BF16), v)
    return m_new, l, acc


def _diff_attn_prompt_kernel(q_ref, k_ref, v_ref, lq1, lk1, lq2, lk2, sub_ref, o_ref, *, tq, lam_init):
    h = pl.program_id(1)
    i = pl.program_id(2)
    scale = A_HEAD_DIM ** -0.5
    q = q_ref[...]
    lane = lax.broadcasted_iota(jnp.int32, (tq, A_QK), 1)
    q1 = jnp.where(lane < A_HEAD_DIM, q, jnp.zeros_like(q))
    q2 = jnp.where(lane >= A_HEAD_DIM, q, jnp.zeros_like(q))
    hh = (h + 1).astype(F32) * (8.0 / A_HEADS)
    slope = jnp.exp2(jnp.zeros((1, tq), F32) - hh)
    qpos = (i * tq).astype(F32) + lax.broadcasted_iota(jnp.int32, (tq, 1), 0).astype(F32)
    kiota = lax.broadcasted_iota(jnp.int32, (1, tq), 1).astype(F32)

    def tile(j, carry, masked):
        m1, l1, a1, m2, l2, a2 = carry
        start = pl.multiple_of(j * tq, tq)
        kj = k_ref[pl.ds(start, tq), :]
        vj = v_ref[pl.ds(start, tq), :]
        dist = qpos - ((j * tq).astype(F32) + kiota)
        bias = slope * dist
        s1 = _dot_nt(q1, kj) * scale - bias
        s2 = _dot_nt(q2, kj) * scale - bias
        if masked:
            ok = dist >= 0.0
            s1 = jnp.where(ok, s1, NEG_BIG)
            s2 = jnp.where(ok, s2, NEG_BIG)
        m1, l1, a1 = _softmax_step(s1, m1, l1, a1, vj)
        m2, l2, a2 = _softmax_step(s2, m2, l2, a2, vj)
        return m1, l1, a1, m2, l2, a2

    z1 = jnp.zeros((tq, 1), F32)
    za = jnp.zeros((tq, A_QK), F32)
    init = (z1 + NEG_BIG, z1, za, z1 + NEG_BIG, z1, za)
    carry = lax.fori_loop(0, i, lambda j, c: tile(j, c, False), init)
    m1, l1, a1, m2, l2, a2 = tile(i, carry, True)
    lam = _lam_value(lq1, lk1, lq2, lk2, lam_init)
    o = a1 / l1 - lam * (a2 / l2)
    o_ref[...] = _subln(o, sub_ref, lam_init).astype(o_ref.dtype)


def diff_attn_prompt(q, k, v, lam_rows, subln, *, batch, seq, lam_init, tq=512):
    nq = seq // tq
    small = [pl.BlockSpec((1, A_HEAD_DIM), lambda b, h, i: (0, 0)) for _ in range(4)]
    return pl.pallas_call(
        functools.partial(_diff_attn_prompt_kernel, tq=tq, lam_init=lam_init),
        out_shape=jax.ShapeDtypeStruct((batch * seq, A_WIDTH), BF16),
        grid=(batch, A_HEADS, nq),
        in_specs=[pl.BlockSpec((tq, A_QK), lambda b, h, i: (b * nq + i, h)),
                  pl.BlockSpec((seq, A_QK), lambda b, h, i: (b, h)),
                  pl.BlockSpec((seq, A_QK), lambda b, h, i: (b, h))] + small
                 + [pl.BlockSpec((1, A_QK), lambda b, h, i: (0, 0))],
        out_specs=pl.BlockSpec((tq, A_QK), lambda b, h, i: (b * nq + i, h)),
        compiler_params=_params(("parallel", "parallel", "arbitrary")),
        name="diff_attn_prompt",
    )(q, k, v, *lam_rows, subln.reshape(1, A_QK))


def _page_pipeline(pt_ref, k_hbm, v_hbm, kbuf, vbuf, sem, *, n_steps, pps, page_of):
    t = pl.program_id(0) * n_steps + pl.program_id(1)
    total = pl.num_programs(0) * n_steps
    depth = PAGE_BUFFERS - 1

    def copies(tt):
        seq, step, slot = tt // n_steps, tt % n_steps, tt % PAGE_BUFFERS
        out = []
        for n in range(pps):
            page = pt_ref[seq, page_of(step, n)]
            out.append(pltpu.make_async_copy(k_hbm.at[page], kbuf.at[slot, n], sem.at[slot, 0, n]))
            out.append(pltpu.make_async_copy(v_hbm.at[page], vbuf.at[slot, n], sem.at[slot, 1, n]))
        return out

    @pl.when(t == 0)
    def _():
        for d in range(depth):
            for c in copies(d):
                c.start()

    @pl.when(t + depth < total)
    def _():
        for c in copies(t + depth):
            c.start()

    for c in copies(t):
        c.wait()
    return t % PAGE_BUFFERS


def _page_pipeline_scratch(pps, n_keys, width):
    return [pltpu.VMEM((PAGE_BUFFERS, pps, n_keys, width), F32), pltpu.VMEM((PAGE_BUFFERS, pps, n_keys, width), F32),
            pltpu.SemaphoreType.DMA((PAGE_BUFFERS, 2, pps))]


def _diff_attn_sample_kernel(pt_ref, q_ref, kn_ref, vn_ref, k_hbm, v_hbm, lq1, lk1, lq2, lk2, sub_ref,
                             o_ref, m_scr, l_scr, acc_scr, kbuf, vbuf, sem, *, n_pages, pps, lam_init):
    slot = _page_pipeline(pt_ref, k_hbm, v_hbm, kbuf, vbuf, sem, n_steps=n_pages // pps, pps=pps,
                          page_of=lambda step, n: step * pps + n)
    p = pl.program_id(1)
    rows = 2 * A_HEADS
    n_keys = PAGE_SIZE * A_HEADS * pps
    scale = A_HEAD_DIM ** -0.5

    @pl.when(p == 0)
    def _():
        m_scr[...] = jnp.full((rows, 1), NEG_BIG, F32)
        l_scr[...] = jnp.zeros((rows, 1), F32)
        acc_scr[...] = jnp.zeros((rows, A_QK), F32)

    q = q_ref[0]
    lane = lax.broadcasted_iota(jnp.int32, (A_HEADS, A_QK), 1)
    qm = jnp.concatenate([jnp.where(lane < A_HEAD_DIM, q, 0.0), jnp.where(lane >= A_HEAD_DIM, q, 0.0)], axis=0)
    row = lax.broadcasted_iota(jnp.int32, (rows, n_keys), 0)
    col = lax.broadcasted_iota(jnp.int32, (rows, n_keys), 1)
    own = (col % A_HEADS) == (row % A_HEADS)
    head1 = (lax.broadcasted_iota(jnp.int32, (rows, 1), 0) % A_HEADS + 1).astype(F32) * (8.0 / A_HEADS)
    slope = jnp.exp2(-head1)
    kpos = (p * (PAGE_SIZE * pps)).astype(F32) + (col // A_HEADS).astype(F32)
    dist = float(n_pages * PAGE_SIZE) - kpos
    qb = qm.astype(BF16)
    s = jnp.concatenate([_dot_nt(qb, kbuf[slot, n].astype(BF16)) for n in range(pps)], axis=1)
    s = jnp.where(own, s * scale - slope * dist, NEG_BIG)
    m0, l0 = m_scr[...], l_scr[...]
    m = jnp.maximum(m0, jnp.max(s, axis=-1, keepdims=True))
    alpha = jnp.exp(m0 - m)
    pr = jnp.exp(s - m)
    l = alpha * l0 + jnp.sum(pr, axis=-1, keepdims=True)
    acc = alpha * acc_scr[...]
    page_keys = PAGE_SIZE * A_HEADS
    for n in range(pps):
        acc = acc + _dot(pr[:, n * page_keys:(n + 1) * page_keys].astype(BF16), vbuf[slot, n].astype(BF16))
    m_scr[...] = m
    l_scr[...] = l
    acc_scr[...] = acc

    @pl.when(p == n_pages // pps - 1)
    def _():
        kn2 = jnp.concatenate([kn_ref[0], kn_ref[0]], axis=0)
        vn2 = jnp.concatenate([vn_ref[0], vn_ref[0]], axis=0)
        s_new = jnp.sum(qm * kn2, axis=-1, keepdims=True) * scale
        m_new = jnp.maximum(m, s_new)
        alpha = jnp.exp(m - m_new)
        p_new = jnp.exp(s_new - m_new)
        o16 = (alpha * acc + p_new * vn2) / (alpha * l + p_new)
        lam = _lam_value(lq1, lk1, lq2, lk2, lam_init)
        o = o16[:A_HEADS] - lam * o16[A_HEADS:]
        o_ref[0] = _subln(o, sub_ref, lam_init).astype(o_ref.dtype)


def diff_attn_sample(q, k_new, v_new, cache_k, cache_v, page_table, lam_rows, subln, *, lam_init):
    s_n, n_pages = page_table.shape
    pps = 4
    assert n_pages % pps == 0
    n_keys = PAGE_SIZE * A_HEADS
    small = [pl.BlockSpec((1, A_HEAD_DIM), lambda s, p, pt: (0, 0)) for _ in range(4)]
    row_spec = pl.BlockSpec((1, A_HEADS, A_QK), lambda s, p, pt: (s, 0, 0))
    hbm = pl.BlockSpec(memory_space=pl.ANY)
    rows = 2 * A_HEADS
    heads = lambda z: z.reshape(s_n, A_HEADS, A_QK)
    out = pl.pallas_call(
        functools.partial(_diff_attn_sample_kernel, n_pages=n_pages, pps=pps, lam_init=lam_init),
        out_shape=jax.ShapeDtypeStruct((s_n, A_HEADS, A_QK), BF16),
        grid_spec=pltpu.PrefetchScalarGridSpec(
            num_scalar_prefetch=1,
            grid=(s_n, n_pages // pps),
            in_specs=[row_spec, row_spec, row_spec, hbm, hbm] + small
                     + [pl.BlockSpec((1, A_QK), lambda s, p, pt: (0, 0))],
            out_specs=pl.BlockSpec((1, A_HEADS, A_QK), lambda s, p, pt: (s, 0, 0)),
            scratch_shapes=[pltpu.VMEM((rows, 1), F32), pltpu.VMEM((rows, 1), F32),
                            pltpu.VMEM((rows, A_QK), F32)] + _page_pipeline_scratch(pps, n_keys, A_QK)),
        compiler_params=_params(("arbitrary", "arbitrary")),
        name="diff_attn_sample",
    )(page_table, heads(q), heads(k_new), heads(v_new), cache_k, cache_v, *lam_rows, subln.reshape(1, A_QK))
    return out.reshape(s_n, A_WIDTH)


def _log_sigmoid(z):
    return jnp.minimum(z, 0.0) - jnp.log1p(jnp.exp(-jnp.abs(z)))


def _suffix_sum(lk, tri):
    hi = lk.astype(BF16)
    lo = (lk - hi.astype(F32)).astype(BF16)
    return _dot(hi, tri) + _dot(lo, tri)


def _strict_lower_ones(n):
    r = lax.broadcasted_iota(jnp.int32, (n, n), 0)
    c = lax.broadcasted_iota(jnp.int32, (n, n), 1)
    return jnp.where(r > c, 1.0, 0.0).astype(BF16)


def _sb_prompt_kernel(q_ref, k_ref, v_ref, o_ref, *, tq, hpg):
    i = pl.program_id(2)
    scale = C_HEAD_DIM ** -0.5
    tri = _strict_lower_ones(tq)
    rr = lax.broadcasted_iota(jnp.int32, (tq, tq), 0)
    cc = lax.broadcasted_iota(jnp.int32, (tq, tq), 1)
    before = cc < rr
    qs = [q_ref[:, h * C_HEAD_DIM:(h + 1) * C_HEAD_DIM] for h in range(hpg)]

    def tile(j, carry, masked):
        start = pl.multiple_of(j * tq, tq)
        out = []
        for h in range(hpg):
            c, acc = carry[h]
            lanes = slice(h * C_HEAD_DIM, (h + 1) * C_HEAD_DIM)
            kj = k_ref[pl.ds(start, tq), lanes]
            vj = v_ref[pl.ds(start, tq), lanes]
            z = _dot_nt(qs[h], kj) * scale
            ls = _log_sigmoid(z)
            lk = ls - z
            if masked:
                lk = jnp.where(before, lk, 0.0)
            later = c + _suffix_sum(lk, tri)
            att = jnp.exp(ls + later)
            if masked:
                att = jnp.where(before, att, 0.0)
            acc = acc + _dot(att.astype(BF16), vj)
            c = c + jnp.sum(lk, axis=-1, keepdims=True)
            out.append((c, acc))
        return tuple(out)

    zero = (jnp.zeros((tq, 1), F32), jnp.zeros((tq, C_HEAD_DIM), F32))
    carry = tile(i, (zero,) * hpg, True)
    carry = lax.fori_loop(0, i, lambda jj, c: tile(i - 1 - jj, c, False), carry)
    o_ref[...] = jnp.concatenate([acc for _, acc in carry], axis=1).astype(o_ref.dtype)


def sb_attn_prompt(q, k, v, *, batch, seq, tq=512, hpg=2):
    nq = seq // tq
    width = C_HEADS * C_HEAD_DIM
    wl = hpg * C_HEAD_DIM
    return pl.pallas_call(
        functools.partial(_sb_prompt_kernel, tq=tq, hpg=hpg),
        out_shape=jax.ShapeDtypeStruct((batch * seq, width), BF16),
        grid=(batch, C_HEADS // hpg, nq),
        in_specs=[pl.BlockSpec((tq, wl), lambda b, h, i: (b * nq + i, h)),
                  pl.BlockSpec((seq, wl), lambda b, h, i: (b, h)),
                  pl.BlockSpec((seq, wl), lambda b, h, i: (b, h))],
        out_specs=pl.BlockSpec((tq, wl), lambda b, h, i: (b * nq + i, h)),
        compiler_params=_params(("parallel", "parallel", "arbitrary")),
        name="sb_attn_prompt",
    )(q, k, v)


def _sb_sample_kernel(pt_ref, q_ref, k_hbm, v_hbm, o_ref, c_scr, acc_scr, kbuf, vbuf, sem, *, n_pages):
    p = pl.program_id(1)
    n_tiles = PAGE_SIZE * C_HEADS // LANES
    scale = C_HEAD_DIM ** -0.5

    @pl.when(p == 0)
    def _():
        c_scr[...] = jnp.zeros((1, LANES), F32)
        acc_scr[...] = jnp.zeros((C_HEADS, C_HEAD_DIM), F32)

    slot = _page_pipeline(pt_ref, k_hbm, v_hbm, kbuf, vbuf, sem, n_steps=n_pages, pps=1,
                          page_of=lambda step, n: n_pages - 1 - step)
    kf = kbuf[slot, 0].astype(BF16)
    vf = vbuf[slot, 0].astype(BF16)
    zt = _dot_nt(q_ref[0].astype(BF16), kf)
    row = lax.broadcasted_iota(jnp.int32, (C_HEADS, LANES), 0)
    lane = lax.broadcasted_iota(jnp.int32, (C_HEADS, LANES), 1)
    own = (lane % C_HEADS) == row
    z = jnp.concatenate([jnp.sum(jnp.where(own, zt[:, g * LANES:(g + 1) * LANES], 0.0), axis=0, keepdims=True)
                         for g in range(n_tiles)], axis=0) * scale
    ls = _log_sigmoid(z)
    lk = ls - z
    li = lax.broadcasted_iota(jnp.int32, (LANES, LANES), 0)
    lj = lax.broadcasted_iota(jnp.int32, (LANES, LANES), 1)
    same = (li % C_HEADS) == (lj % C_HEADS)
    hi = lk.astype(BF16)
    lo = (lk - hi.astype(F32)).astype(BF16)
    sel = jnp.concatenate([jnp.where(same & (li > lj), 1.0, 0.0), jnp.where(same, 1.0, 0.0)], axis=1).astype(BF16)
    both = _dot(hi, sel) + _dot(lo, sel)
    inside, total = both[:, :LANES], both[:, LANES:]
    run = c_scr[...]
    later = [None] * n_tiles
    for g in range(n_tiles - 1, -1, -1):
        later[g] = run + inside[g:g + 1]
        run = run + total[g:g + 1]
    c_scr[...] = run
    att = jnp.exp(ls + jnp.concatenate(later, axis=0))
    spread = jnp.concatenate([jnp.where(own, att[g:g + 1], 0.0) for g in range(n_tiles)], axis=1)
    acc = acc_scr[...] + _dot(spread.astype(BF16), vf)
    acc_scr[...] = acc

    @pl.when(p == pl.num_programs(1) - 1)
    def _():
        o_ref[0] = acc.astype(o_ref.dtype)


def sb_attn_sample(q, cache_k, cache_v, page_table):
    s_n, n_pages = page_table.shape
    n_keys = PAGE_SIZE * C_HEADS
    hbm = pl.BlockSpec(memory_space=pl.ANY)
    out = pl.pallas_call(
        functools.partial(_sb_sample_kernel, n_pages=n_pages),
        out_shape=jax.ShapeDtypeStruct((s_n, C_HEADS, C_HEAD_DIM), BF16),
        grid_spec=pltpu.PrefetchScalarGridSpec(
            num_scalar_prefetch=1,
            grid=(s_n, n_pages),
            in_specs=[pl.BlockSpec((1, C_HEADS, C_HEAD_DIM), lambda s, p, pt: (s, 0, 0)), hbm, hbm],
            out_specs=pl.BlockSpec((1, C_HEADS, C_HEAD_DIM), lambda s, p, pt: (s, 0, 0)),
            scratch_shapes=[pltpu.VMEM((1, LANES), F32), pltpu.VMEM((C_HEADS, C_HEAD_DIM), F32)]
                           + _page_pipeline_scratch(1, n_keys, C_HEAD_DIM)),
        compiler_params=_params(("arbitrary", "arbitrary")),
        name="sb_attn_sample",
    )(page_table, q.reshape(s_n, C_HEADS, C_HEAD_DIM), cache_k, cache_v)
    return out.reshape(s_n, C_HEADS * C_HEAD_DIM)


def _head_sum(x):
    r = lax.broadcasted_iota(jnp.int32, (LANES, LANES), 0) // B_HEAD
    c = lax.broadcasted_iota(jnp.int32, (LANES, LANES), 1) // B_HEAD
    ones = jnp.where(r == c, 1.0, 0.0).astype(F32)
    parts = [_dot_f32(x[:, g * LANES:(g + 1) * LANES], ones) for g in range(x.shape[1] // LANES)]
    return jnp.concatenate(parts, axis=1)


def _softplus(x):
    return jnp.maximum(x, 0.0) + jnp.log1p(jnp.exp(-jnp.abs(x)))


def _sigmoid(x):
    return 1.0 / (1.0 + jnp.exp(-x))


def _rwkv_prep_kernel(p_ref, prev_ref, mu_ref, w0_ref, a0_ref, kk_ref, ka_ref, rk_ref, w2_ref, a2_ref, g2_ref,
                      *refs, sequential):
    outs = refs[:10]
    p = p_ref[...]
    tm = p.shape[0]
    if sequential:
        last_scr = refs[10]
        t = pl.program_id(1)

        @pl.when(t == 0)
        def _():
            last_scr[...] = prev_ref[0]

        rolled = pltpu.roll(p, 1, axis=0)
        row = lax.broadcasted_iota(jnp.int32, (tm, 1), 0)
        shifted = jnp.where(row == 0, last_scr[...], rolled)
        last_scr[...] = p[tm - 1:tm, :]
    else:
        shifted = prev_ref[...]
    xs = p + (shifted - p) * mu_ref[...]
    r = xs[:, :B_WIDTH]
    k = xs[:, B_WIDTH:2 * B_WIDTH]
    v = xs[:, 2 * B_WIDTH:3 * B_WIDTH]
    tail = xs[:, 3 * B_WIDTH:]
    low_rank = lambda z, w_ref: _dot(z.astype(BF16), w_ref[...].astype(BF16))
    w = -_softplus(-(w0_ref[...] + low_rank(jnp.tanh(tail), w2_ref))) - 0.5
    decay = jnp.exp(-jnp.exp(w))
    a = _sigmoid(a0_ref[...] + low_rank(tail, a2_ref))
    g = low_rank(_sigmoid(tail), g2_ref)
    kk = k * kk_ref[...]
    kk = kk * lax.rsqrt(jnp.maximum(_head_sum(kk * kk), 1e-24))
    k_h = k * (1.0 + (a - 1.0) * ka_ref[...])
    bonus = _head_sum(r * k_h * rk_ref[...]) * v
    b = kk * a
    r_o, d_o, k_o, na_o, b_o, v_o, g_o, bonus_o, yw_o, vkr_o = outs
    r_o[0] = r
    d_o[0] = decay
    k_o[0] = k_h
    na_o[0] = -kk
    b_o[0] = b
    v_o[0] = v
    g_o[0] = g
    bonus_o[0] = bonus
    yw_o[0] = decay * r - kk * _head_sum(b * r)
    vkr_o[0] = v * _head_sum(k_h * r)


def rwkv_prep(p, prev, weights, *, batch, seq, tm, sequential):
    nt = seq // tm if sequential else 1
    nb = batch if sequential else (batch * seq) // tm
    vec = lambda n: pl.BlockSpec((1, n), lambda b, t: (0, 0))
    mat = lambda: pl.BlockSpec((B_TAIL, B_WIDTH), lambda b, t: (0, 0))
    if sequential:
        p_spec = pl.BlockSpec((tm, B_PROJ_PAD), lambda b, t: (b * nt + t, 0))
        prev_spec = pl.BlockSpec((1, 1, B_PROJ_PAD), lambda b, t: (b, 0, 0))
        n_shape = jax.ShapeDtypeStruct((batch, seq, B_WIDTH), F32)
        n_spec = pl.BlockSpec((1, tm, B_WIDTH), lambda b, t: (b, t, 0))
        scratch = [pltpu.VMEM((1, B_PROJ_PAD), F32)]
    else:
        p_spec = pl.BlockSpec((tm, B_PROJ_PAD), lambda b, t: (b, 0))
        prev_spec = pl.BlockSpec((tm, B_PROJ_PAD), lambda b, t: (b, 0))
        n_shape = jax.ShapeDtypeStruct((nb, tm, B_WIDTH), F32)
        n_spec = pl.BlockSpec((1, tm, B_WIDTH), lambda b, t: (b, 0, 0))
        scratch = []
    return pl.pallas_call(
        functools.partial(_rwkv_prep_kernel, sequential=sequential),
        out_shape=(n_shape,) * 10,
        grid=(nb, nt),
        in_specs=[p_spec, prev_spec, vec(B_PROJ_PAD), vec(B_WIDTH), vec(B_WIDTH), vec(B_WIDTH), vec(B_WIDTH),
                  vec(B_WIDTH), mat(), mat(), mat()],
        out_specs=(n_spec,) * 10,
        scratch_shapes=scratch,
        compiler_params=_params(("parallel", "arbitrary")),
        name="rwkv_prep",
    )(p, prev, *weights)


def _rwkv_scan_kernel(a_ref, d_ref, b_ref, k_ref, v_ref, yw_ref, vkr_ref, y_ref, s_ref, st_scr, *, nb, tc, n_groups):
    t_idx = pl.program_id(1)
    n_pairs = B_HEADS // 2

    @pl.when(t_idx == 0)
    def _():
        st_scr[...] = jnp.zeros(st_scr.shape, F32)

    r_i = lax.broadcasted_iota(jnp.int32, (B_HEAD, LANES), 0)
    l_i = lax.broadcasted_iota(jnp.int32, (B_HEAD, LANES), 1)
    diag = (l_i % B_HEAD) == r_i
    kr = lax.broadcasted_iota(jnp.int32, (LANES, LANES), 0) // B_HEAD
    kc = lax.broadcasted_iota(jnp.int32, (LANES, LANES), 1) // B_HEAD
    ones_blk = jnp.where(kr == kc, 1.0, 0.0).astype(BF16)
    chains = [(bb, pr) for bb in range(nb) for pr in range(n_pairs)]
    per = len(chains) // n_groups
    groups = [chains[g * per:(g + 1) * per] for g in range(n_groups)]
    refs = dict(a=a_ref, d=d_ref, b=b_ref, k=k_ref, v=v_ref, yw=yw_ref, vkr=vkr_ref)

    def block8(blk, carry):
        t0 = pl.multiple_of(blk * SUBLANES, SUBLANES)
        rows8 = {n: [ref[bb, pl.ds(t0, SUBLANES), :] for bb in range(nb)] for n, ref in refs.items()}
        y_rows = [[[None] * n_pairs for _ in range(SUBLANES)] for _ in range(nb)]
        for cc in range(SUBLANES):
            for grp in groups:
                row = lambda n, bb, pr: rows8[n][bb][cc:cc + 1, pr * LANES:(pr + 1) * LANES]
                lhs = []
                for bb, pr in grp:
                    st = st_scr[bb * n_pairs + pr]
                    lhs.append((st * row("a", bb, pr)).astype(BF16))
                    lhs.append(jnp.where(diag, row("v", bb, pr), 0.0).astype(BF16))
                    lhs.append((st * row("yw", bb, pr) + jnp.where(diag, row("vkr", bb, pr), 0.0)).astype(BF16))
                res = _dot(jnp.concatenate(lhs, axis=0), ones_blk)
                for n, (bb, pr) in enumerate(grp):
                    base = 3 * n * B_HEAD
                    sa_b = res[base:base + B_HEAD]
                    v_b = res[base + B_HEAD:base + 2 * B_HEAD]
                    y_b = res[base + 2 * B_HEAD:base + 3 * B_HEAD]
                    st = st_scr[bb * n_pairs + pr]
                    st_scr[bb * n_pairs + pr] = st * row("d", bb, pr) + sa_b * row("b", bb, pr) + v_b * row("k", bb, pr)
                    y_rows[bb][cc][pr] = jnp.sum(jnp.where(diag, y_b, 0.0), axis=0, keepdims=True)
        for bb in range(nb):
            y8 = [jnp.concatenate(y_rows[bb][cc], axis=1) for cc in range(SUBLANES)]
            y_ref[bb, pl.ds(t0, SUBLANES), :] = jnp.concatenate(y8, axis=0)
        return carry

    lax.fori_loop(0, tc // SUBLANES, block8, 0)

    @pl.when(t_idx == pl.num_programs(1) - 1)
    def _():
        for bb, pr in chains:
            st = st_scr[bb * n_pairs + pr]
            for half in range(2):
                s_ref[bb, 2 * pr + half] = st[:, half * B_HEAD:(half + 1) * B_HEAD]


def rwkv_scan(na, d, b, k, v, yw, vkr, *, nb=2, tc=128, n_groups=1):
    batch, seq, _ = v.shape
    spec = pl.BlockSpec((nb, tc, B_WIDTH), lambda g, t: (g, t, 0))
    return pl.pallas_call(
        functools.partial(_rwkv_scan_kernel, nb=nb, tc=tc, n_groups=n_groups),
        out_shape=(jax.ShapeDtypeStruct((batch, seq, B_WIDTH), F32),
                   jax.ShapeDtypeStruct((batch, B_HEADS, B_HEAD, B_HEAD), F32)),
        grid=(batch // nb, seq // tc),
        in_specs=[spec] * 7,
        out_specs=(spec, pl.BlockSpec((nb, B_HEADS, B_HEAD, B_HEAD), lambda g, t: (g, 0, 0, 0))),
        scratch_shapes=[pltpu.VMEM((nb * B_HEADS // 2, B_HEAD, LANES), F32)],
        compiler_params=_params(("parallel", "arbitrary")),
        name="rwkv_scan",
    )(na, d, b, k, v, yw, vkr)


def _rwkv_step_kernel(a_ref, r_ref, d_ref, b_ref, k_ref, v_ref, s0_ref, y_ref, s_ref):
    r_i = lax.broadcasted_iota(jnp.int32, (B_HEAD, B_HEAD), 0)
    c_i = lax.broadcasted_iota(jnp.int32, (B_HEAD, B_HEAD), 1)
    eye = jnp.where(r_i == c_i, 1.0, 0.0).astype(F32)
    for h in range(B_HEADS):
        row = lambda ref: ref[0, h:h + 1, :]
        st = s0_ref[0, h]
        sa = jnp.sum(st * row(a_ref), axis=1, keepdims=True)
        v_col = jnp.sum(eye * row(v_ref), axis=1, keepdims=True)
        st = st * row(d_ref) + sa * row(b_ref) + v_col * row(k_ref)
        s_ref[0, h] = st
        y_col = jnp.sum(st * row(r_ref), axis=1, keepdims=True)
        y_ref[0, h:h + 1, :] = jnp.sum(eye * y_col, axis=0, keepdims=True)


def rwkv_step(na, r, d, b, k, v, s0):
    s_n = s0.shape[0]
    hs = lambda z: z.reshape(s_n, B_HEADS, B_HEAD)
    vec = pl.BlockSpec((1, B_HEADS, B_HEAD), lambda s: (s, 0, 0))
    st = pl.BlockSpec((1, B_HEADS, B_HEAD, B_HEAD), lambda s: (s, 0, 0, 0))
    y, s_new = pl.pallas_call(
        _rwkv_step_kernel,
        out_shape=(jax.ShapeDtypeStruct((s_n, B_HEADS, B_HEAD), F32),
                   jax.ShapeDtypeStruct(s0.shape, F32)),
        grid=(s_n,),
        in_specs=[vec] * 6 + [st],
        out_specs=(vec, st),
        compiler_params=_params(("parallel",)),
        name="rwkv_step",
    )(hs(na), hs(r), hs(d), hs(b), hs(k), hs(v), s0)
    return y.reshape(s_n, B_WIDTH), s_new


def _rwkv_post_kernel(y_ref, bonus_ref, g_ref, lw_ref, lb_ref, o_ref):
    y = y_ref[...]
    mean = _head_sum(y) * (1.0 / B_HEAD)
    yc = y - mean
    var = _head_sum(yc * yc) * (1.0 / B_HEAD)
    out = yc * lax.rsqrt(var + GN_EPS) * lw_ref[...] + lb_ref[...]
    o_ref[...] = ((out + bonus_ref[...]) * g_ref[...]).astype(o_ref.dtype)


def rwkv_post(y, bonus, g, ln_w, ln_b, *, tm):
    m = y.shape[0]
    row = pl.BlockSpec((tm, B_WIDTH), lambda i: (i, 0))
    vec = pl.BlockSpec((1, B_WIDTH), lambda i: (0, 0))
    return pl.pallas_call(
        _rwkv_post_kernel,
        out_shape=jax.ShapeDtypeStruct((m, B_WIDTH), BF16),
        grid=(m // tm,),
        in_specs=[row, row, row, vec, vec],
        out_specs=row,
        compiler_params=_params(("parallel",)),
        name="rwkv_post",
    )(y, bonus, g, ln_w.reshape(1, B_WIDTH), ln_b.reshape(1, B_WIDTH))


def _router_kernel(x_ref, g_ref, w_ref, b_ref, id_ref, wt_ref):
    x = x_ref[...]
    tm = x.shape[0]
    h = (x * lax.rsqrt(jnp.mean(x * x, axis=-1, keepdims=True) + RMS_EPS)) * g_ref[...]
    logits = _dot(h.astype(BF16), w_ref[...].astype(BF16)) + b_ref[...]
    lane = lax.broadcasted_iota(jnp.int32, (tm, LANES), 1).astype(F32)
    none = float(LANES)
    is_grp = lane < N_GROUPS
    g_max = jnp.max(jnp.where(is_grp, logits, NEG_BIG), axis=-1, keepdims=True)
    grp = jnp.min(jnp.where(is_grp & (logits == g_max), lane, none), axis=-1, keepdims=True)
    p_grp = 1.0 / jnp.sum(jnp.where(is_grp, jnp.exp(logits - g_max), 0.0), axis=-1, keepdims=True)
    lo = N_GROUPS + EXPERTS_PER_GROUP * grp
    in_grp = (lane >= lo) & (lane < lo + EXPERTS_PER_GROUP)
    v1 = jnp.max(jnp.where(in_grp, logits, NEG_BIG), axis=-1, keepdims=True)
    i1 = jnp.min(jnp.where(in_grp & (logits == v1), lane, none), axis=-1, keepdims=True)
    rest = in_grp & (lane != i1)
    v2 = jnp.max(jnp.where(rest, logits, NEG_BIG), axis=-1, keepdims=True)
    i2 = jnp.min(jnp.where(rest & (logits == v2), lane, none), axis=-1, keepdims=True)
    e = jnp.exp(v2 - v1)
    w1 = p_grp / (1.0 + e)
    w2 = p_grp * e / (1.0 + e)
    ids = jnp.where(lane == 0.0, i1 - N_GROUPS, jnp.where(lane == 1.0, i2 - N_GROUPS, 0.0))
    id_ref[...] = ids.astype(jnp.int32)
    wt_ref[...] = jnp.where(lane == 0.0, w1, jnp.where(lane == 1.0, w2, 0.0))


def moe_router(x, gain, w_router, b_router, *, tm):
    m, k = x.shape
    return pl.pallas_call(
        _router_kernel,
        out_shape=(jax.ShapeDtypeStruct((m, LANES), jnp.int32), jax.ShapeDtypeStruct((m, LANES), F32)),
        grid=(m // tm,),
        in_specs=[pl.BlockSpec((tm, k), lambda i: (i, 0)), pl.BlockSpec((1, k), lambda i: (0, 0)),
                  pl.BlockSpec((k, LANES), lambda i: (0, 0)), pl.BlockSpec((1, LANES), lambda i: (0, 0))],
        out_specs=(pl.BlockSpec((tm, LANES), lambda i: (i, 0)), pl.BlockSpec((tm, LANES), lambda i: (i, 0))),
        compiler_params=_params(("parallel",)),
        name="moe_router",
    )(x, gain.reshape(1, k), w_router, b_router)


def _dispatch_kernel(src_ref, nrow_ref, g_ref, x_hbm, o_ref, buf, sem, *, tm):
    i = pl.program_id(0)
    live = i * tm < nrow_ref[0]

    def row_copy(tile, k):
        slot = tile % 2
        return pltpu.make_async_copy(x_hbm.at[pl.ds(src_ref[tile * tm + k], 1)], buf.at[slot, pl.ds(k, 1)],
                                     sem.at[slot])

    def start_tile(tile):
        def start(blk, carry):
            for j in range(DMA_LOOP_UNROLL):
                row_copy(tile, blk * DMA_LOOP_UNROLL + j).start(priority=j % 2)
            return carry

        lax.fori_loop(0, tm // DMA_LOOP_UNROLL, start, 0)

    @pl.when(i == 0)
    def _():
        start_tile(0)

    @pl.when((i + 1) * tm < nrow_ref[0])
    def _():
        start_tile(i + 1)

    @pl.when(live)
    def _():
        def wait(k, carry):
            row_copy(i, k).wait()
            return carry

        lax.fori_loop(0, tm, wait, 0, unroll=DMA_LOOP_UNROLL)
        x = buf[i % 2]
        h = (x * lax.rsqrt(jnp.mean(x * x, axis=-1, keepdims=True) + RMS_EPS)) * g_ref[...]
        o_ref[...] = h.astype(o_ref.dtype)

    @pl.when(jnp.logical_not(live))
    def _():
        o_ref[...] = jnp.zeros(o_ref.shape, o_ref.dtype)


def moe_dispatch(x_all, gain, src, n_rows_used, *, tm):
    rows = src.shape[0]
    k = x_all.shape[1]
    return pl.pallas_call(
        functools.partial(_dispatch_kernel, tm=tm),
        out_shape=jax.ShapeDtypeStruct((rows, k), BF16),
        grid_spec=pltpu.PrefetchScalarGridSpec(
            num_scalar_prefetch=2,
            grid=(rows // tm,),
            in_specs=[pl.BlockSpec((1, k), lambda i, s, n: (0, 0)), pl.BlockSpec(memory_space=pl.ANY)],
            out_specs=pl.BlockSpec((tm, k), lambda i, s, n: (i, 0)),
            scratch_shapes=[pltpu.VMEM((2, tm, k), F32), pltpu.SemaphoreType.DMA((2,))]),
        compiler_params=_params(("arbitrary",)),
        name="moe_dispatch",
    )(src, n_rows_used, gain.reshape(1, k), x_all)


def _expert_kernel(te_ref, nu_ref, x_ref, wg_ref, wu_ref, wd_ref, o_ref):
    i = pl.program_id(0)
    f = pl.program_id(1)
    used = i < nu_ref[0]

    @pl.when(used)
    def _():
        x = x_ref[...]
        hg = _dot(x, wg_ref[0, 0].astype(BF16))
        hu = _dot(x, wu_ref[0, 0].astype(BF16))
        act = (hg * _sigmoid(hg)) * hu
        part = _dot(act.astype(BF16), wd_ref[0, 0].astype(BF16))

        @pl.when(f == 0)
        def _():
            o_ref[...] = part

        @pl.when(f > 0)
        def _():
            o_ref[...] += part

    @pl.when(jnp.logical_not(used) & (f == 0))
    def _():
        o_ref[...] = jnp.zeros(o_ref.shape, F32)


def moe_experts(xs, tile_expert, n_used, w_gate, w_up, w_down, *, layer, tm, fc):
    rows, d = xs.shape
    n_tiles = rows // tm
    nf = EXPERT_HIDDEN // fc

    def live(i, f, te, nu):
        ok = i < nu[0]
        return jnp.where(ok, i, nu[0] - 1), jnp.where(ok, f, nf - 1)

    def x_map(i, f, te, nu):
        return live(i, f, te, nu)[0], 0

    def gu_map(i, f, te, nu):
        ii, ff = live(i, f, te, nu)
        return layer, te[ii], 0, ff

    def d_map(i, f, te, nu):
        ii, ff = live(i, f, te, nu)
        return layer, te[ii], ff, 0

    return pl.pallas_call(
        _expert_kernel,
        out_shape=jax.ShapeDtypeStruct((rows, d), F32),
        grid_spec=pltpu.PrefetchScalarGridSpec(
            num_scalar_prefetch=2,
            grid=(n_tiles, nf),
            in_specs=[pl.BlockSpec((tm, d), x_map),
                      pl.BlockSpec((1, 1, d, fc), gu_map),
                      pl.BlockSpec((1, 1, d, fc), gu_map),
                      pl.BlockSpec((1, 1, fc, d), d_map)],
            out_specs=pl.BlockSpec((tm, d), lambda i, f, te, nu: (i, 0))),
        compiler_params=_params(("arbitrary", "arbitrary"), vmem=56 * 1024 * 1024),
        name="moe_experts",
    )(tile_expert, n_used, xs, w_gate, w_up, w_down)


def _combine_kernel(dest_ref, x_ref, wt_ref, y_hbm, o_ref, buf, sem, *, tm, n_tok, tok0):
    i = pl.program_id(0)

    def row_copy(k, choice):
        src_row = dest_ref[choice * n_tok + tok0 + i * tm + k]
        return pltpu.make_async_copy(y_hbm.at[pl.ds(src_row, 1)], buf.at[choice, pl.ds(k, 1)], sem.at[choice])

    for choice in range(TOP_K):
        def start(blk, carry, choice=choice):
            for j in range(DMA_LOOP_UNROLL):
                row_copy(blk * DMA_LOOP_UNROLL + j, choice).start(priority=j % 2)
            return carry

        lax.fori_loop(0, tm // DMA_LOOP_UNROLL, start, 0)
    for choice in range(TOP_K):
        def wait(k, carry, choice=choice):
            row_copy(k, choice).wait()
            return carry

        lax.fori_loop(0, tm, wait, 0, unroll=DMA_LOOP_UNROLL)
    wt = wt_ref[...]
    o_ref[...] = x_ref[...] + wt[:, 0:1] * buf[0] + wt[:, 1:2] * buf[1]


def moe_combine(x, y, dest_t, wts, *, tm, tok0):
    m, k = x.shape
    n_tok = dest_t.shape[0] // TOP_K
    row = pl.BlockSpec((tm, k), lambda i, dest: (i, 0))
    return pl.pallas_call(
        functools.partial(_combine_kernel, tm=tm, n_tok=n_tok, tok0=tok0),
        out_shape=jax.ShapeDtypeStruct((m, k), F32),
        grid_spec=pltpu.PrefetchScalarGridSpec(
            num_scalar_prefetch=1,
            grid=(m // tm,),
            in_specs=[row, pl.BlockSpec((tm, LANES), lambda i, dest: (i, 0)), pl.BlockSpec(memory_space=pl.ANY)],
            out_specs=row,
            scratch_shapes=[pltpu.VMEM((TOP_K, tm, k), F32), pltpu.SemaphoreType.DMA((TOP_K,))]),
        compiler_params=_params(("arbitrary",)),
        name="moe_combine",
    )(dest_t, x, wts, y)


def _routing_tables(ids, *, tm):
    n_assign = ids.shape[0] * TOP_K
    n_tiles = (n_assign + N_EXPERTS * (tm - 1)) // tm
    flat = ids.reshape(-1)
    onehot = (flat[:, None] == jnp.arange(N_EXPERTS, dtype=jnp.int32)[None, :]).astype(jnp.int32)
    counts = jnp.sum(onehot, axis=0)
    rank = jnp.take_along_axis(jnp.cumsum(onehot, axis=0), flat[:, None], axis=1)[:, 0] - 1
    tiles_per = (counts + tm - 1) // tm
    tile_end = jnp.cumsum(tiles_per)
    row_start = (tile_end - tiles_per) * tm
    dest = row_start[flat] + rank
    n_used = tile_end[-1]
    tile_ids = jnp.arange(n_tiles, dtype=jnp.int32)
    tile_expert = jnp.minimum(jnp.searchsorted(tile_end, tile_ids, side="right"), N_EXPERTS - 1).astype(jnp.int32)
    last_used = tile_expert[jnp.maximum(n_used - 1, 0)]
    tile_expert = jnp.where(tile_ids < n_used, tile_expert, last_used)
    src = jnp.zeros((n_tiles * tm,), jnp.int32).at[dest].set(jnp.arange(n_assign, dtype=jnp.int32) // TOP_K)
    return src, dest.reshape(-1, TOP_K), tile_expert, n_used.reshape(1).astype(jnp.int32)


def hier_moe_layer(xp, xs, layer, norm_ffn, router_w, router_b, w_gate, w_up, w_down, *, tm_e=512, fc=512):
    idp, wtp = moe_router(xp, norm_ffn[layer], router_w, router_b, tm=512)
    ids_, wts_ = moe_router(xs, norm_ffn[layer], router_w, router_b, tm=xs.shape[0])
    ids = jnp.concatenate([idp[:, :TOP_K], ids_[:, :TOP_K]], axis=0)
    src, dest, tile_expert, n_used = _routing_tables(ids, tm=tm_e)
    x_sorted = moe_dispatch(jnp.concatenate([xp, xs], axis=0), norm_ffn[layer], src, n_used * tm_e, tm=256)
    y = moe_experts(x_sorted, tile_expert, n_used, w_gate, w_up, w_down, layer=layer, tm=tm_e, fc=fc)
    dest_t = dest.T.reshape(-1)
    xp = moe_combine(xp, y, dest_t, wtp, tm=256, tok0=0)
    xs = moe_combine(xs, y, dest_t, wts_, tm=xs.shape[0], tok0=xp.shape[0])
    return xp, xs


def _pad_cols(w, n):
    return jnp.pad(w, ((0, 0), (0, n - w.shape[1])))


def kernel(x_prompt, x_sample, cache_a_k, cache_a_v, state_rwkv, state_shift, cache_sb_k, cache_sb_v, page_table,
           norm_mix, norm_ffn, norm_final, w_in0, lam_q1, lam_k1, lam_q2, lam_k2, subln0,
           rw_mu, rw_w0, rw_w2, rw_a0, rw_a2, rw_g2, rw_k_k, rw_k_a, rw_r_k, rw_ln_w, rw_ln_b, w_out0,
           w_qkv1, w_out1, router_grp_w, router_grp_b, router_exp_w, router_exp_b,
           exp_w_gate, exp_w_up, exp_w_down):
    batch, seq, d = x_prompt.shape
    s_n = x_sample.shape[0]
    n_p = batch * seq
    xp = x_prompt.reshape(n_p, d)
    xs = x_sample.reshape(s_n, d)
    n_phys = cache_a_k.shape[0]
    tm_p = 512

    lam_init = 0.8 - 0.6 * math.exp(-0.3 * 0)
    lam_rows = [z.reshape(1, A_HEAD_DIM) for z in (lam_q1, lam_k1, lam_q2, lam_k2)]
    w_q = w_in0[:, :A_WIDTH].astype(BF16)
    w_k = w_in0[:, A_WIDTH:2 * A_WIDTH].astype(BF16)
    w_v = w_in0[:, 2 * A_WIDTH:3 * A_WIDTH].astype(BF16)
    w_p = _pad_cols(w_in0[:, 3 * A_WIDTH:], B_PROJ_PAD).astype(BF16)
    g0 = norm_mix[0]

    (qp16,) = norm_matmul(xp, g0, w_q, (BF16,), tm=tm_p, tn=512)
    kp32, kp16 = norm_matmul(xp, g0, w_k, (F32, BF16), tm=tm_p, tn=512)
    vp32, vp16 = norm_matmul(xp, g0, w_v, (F32, BF16), tm=tm_p, tn=512)
    (pp,) = norm_matmul(xp, g0, w_p, (F32,), tm=tm_p, tn=1152)
    (qs32,) = norm_matmul(xs, g0, w_q, (F32,), tm=s_n, tn=512)
    (ks32,) = norm_matmul(xs, g0, w_k, (F32,), tm=s_n, tn=512)
    (vs32,) = norm_matmul(xs, g0, w_v, (F32,), tm=s_n, tn=512)
    (ps,) = norm_matmul(xs, g0, w_p, (F32,), tm=s_n, tn=1152)

    oa_p = diff_attn_prompt(qp16, kp16, vp16, lam_rows, subln0, batch=batch, seq=seq, lam_init=lam_init)
    oa_s = diff_attn_sample(qs32, ks32, vs32, cache_a_k.reshape(n_phys, PAGE_SIZE * A_HEADS, A_QK),
                            cache_a_v.reshape(n_phys, PAGE_SIZE * A_HEADS, A_QK), page_table, lam_rows, subln0,
                            lam_init=lam_init)

    zeros_tail = jnp.zeros((B_TAIL - B_DECAY_LORA - B_A_LORA - B_G_LORA, B_WIDTH), F32)
    w2_pad = jnp.concatenate([rw_w2, jnp.zeros((B_A_LORA + B_G_LORA, B_WIDTH), F32), zeros_tail], axis=0)
    a2_pad = jnp.concatenate([jnp.zeros((B_DECAY_LORA, B_WIDTH), F32), rw_a2,
                              jnp.zeros((B_G_LORA, B_WIDTH), F32), zeros_tail], axis=0)
    g2_pad = jnp.concatenate([jnp.zeros((B_DECAY_LORA + B_A_LORA, B_WIDTH), F32), rw_g2, zeros_tail], axis=0)
    row = lambda z: z.reshape(1, -1)
    prep_w = (row(jnp.pad(rw_mu, (0, B_PROJ_PAD - B_PROJ))), row(rw_w0), row(rw_a0), row(rw_k_k), row(rw_k_a),
              row(rw_r_k), w2_pad, a2_pad, g2_pad)

    _, d_n, k_n, na_n, b_n, v_n, g_n, bonus_n, yw_n, vkr_n = rwkv_prep(
        pp, jnp.zeros((batch, 1, B_PROJ_PAD), F32), prep_w, batch=batch, seq=seq, tm=256, sequential=True)
    y_p, rwkv_p = rwkv_scan(na_n, d_n, b_n, k_n, v_n, yw_n, vkr_n)
    ob_p = rwkv_post(y_p.reshape(n_p, B_WIDTH), bonus_n.reshape(n_p, B_WIDTH), g_n.reshape(n_p, B_WIDTH),
                     rw_ln_w, rw_ln_b, tm=tm_p)

    prev_s = jnp.pad(state_shift.reshape(s_n, B_PROJ), ((0, 0), (0, B_PROJ_PAD - B_PROJ)))
    r_s, d_s, k_s, na_s, b_s, v_s, g_s, bonus_s = (z.reshape(s_n, B_WIDTH) for z in rwkv_prep(
        ps, prev_s, prep_w, batch=s_n, seq=1, tm=s_n, sequential=False)[:8])
    y_s, rwkv_s = rwkv_step(na_s, r_s, d_s, b_s, k_s, v_s, state_rwkv)
    ob_s = rwkv_post(y_s, bonus_s, g_s, rw_ln_w, rw_ln_b, tm=s_n)

    w_o0 = w_out0.astype(BF16)
    xp = matmul_residual([oa_p, ob_p], w_o0, xp, tm=tm_p, tn=512)
    xs = matmul_residual([oa_s, ob_s], w_o0, xs, tm=s_n, tn=512)

    shift_p = pp.reshape(batch, seq, B_PROJ_PAD)[:, seq - 1:, :B_PROJ]
    shift_s = ps[:, :B_PROJ].reshape(s_n, 1, B_PROJ)
    a_k_p = kp32.reshape(batch, seq, A_HEADS, A_QK)
    a_v_p = vp32.reshape(batch, seq, A_HEADS, A_QK)
    a_k_s = ks32.reshape(s_n, 1, A_HEADS, A_QK)
    a_v_s = vs32.reshape(s_n, 1, A_HEADS, A_QK)

    def router_weights(layer):
        w = jnp.concatenate([router_grp_w[layer], router_exp_w[layer]], axis=1)
        b = jnp.concatenate([router_grp_b[layer], router_exp_b[layer]], axis=0)
        return _pad_cols(w, LANES), jnp.pad(b, (0, LANES - b.shape[0])).reshape(1, LANES)

    rw0, rb0 = router_weights(0)
    xp, xs = hier_moe_layer(xp, xs, 0, norm_ffn, rw0, rb0, exp_w_gate, exp_w_up, exp_w_down)

    g1 = norm_mix[1]
    w_q1 = w_qkv1[:, :d].astype(BF16)
    w_k1 = w_qkv1[:, d:2 * d].astype(BF16)
    w_v1 = w_qkv1[:, 2 * d:].astype(BF16)
    (qp16,) = norm_matmul(xp, g1, w_q1, (BF16,), tm=tm_p, tn=512)
    kp32, kp16 = norm_matmul(xp, g1, w_k1, (F32, BF16), tm=tm_p, tn=512)
    vp32, vp16 = norm_matmul(xp, g1, w_v1, (F32, BF16), tm=tm_p, tn=512)
    (qs32,) = norm_matmul(xs, g1, w_q1, (F32,), tm=s_n, tn=512)
    (ks32,) = norm_matmul(xs, g1, w_k1, (F32,), tm=s_n, tn=512)
    (vs32,) = norm_matmul(xs, g1, w_v1, (F32,), tm=s_n, tn=512)

    oc_p = sb_attn_prompt(qp16, kp16, vp16, batch=batch, seq=seq)
    oc_s = sb_attn_sample(qs32, cache_sb_k.reshape(n_phys, PAGE_SIZE * C_HEADS, C_HEAD_DIM),
                          cache_sb_v.reshape(n_phys, PAGE_SIZE * C_HEADS, C_HEAD_DIM), page_table)
    w_o1 = w_out1.astype(BF16)
    xp = matmul_residual([oc_p], w_o1, xp, tm=tm_p, tn=512)
    xs = matmul_residual([oc_s], w_o1, xs, tm=s_n, tn=512)

    sb_k_p = kp32.reshape(batch, seq, C_HEADS, C_HEAD_DIM)
    sb_v_p = vp32.reshape(batch, seq, C_HEADS, C_HEAD_DIM)
    sb_k_s = ks32.reshape(s_n, 1, C_HEADS, C_HEAD_DIM)
    sb_v_s = vs32.reshape(s_n, 1, C_HEADS, C_HEAD_DIM)

    rw1, rb1 = router_weights(1)
    xp, xs = hier_moe_layer(xp, xs, 1, norm_ffn, rw1, rb1, exp_w_gate, exp_w_up, exp_w_down)

    y_prompt = rmsnorm_rows(xp, norm_final, tm=tm_p).reshape(batch, seq, d)
    y_sample = rmsnorm_rows(xs, norm_final, tm=s_n).reshape(s_n, 1, d)
    return (y_prompt, y_sample, a_k_p, a_v_p, a_k_s, a_v_s, rwkv_p, rwkv_s, shift_p, shift_s,
            sb_k_p, sb_v_p, sb_k_s, sb_v_s)
```
